```python
import math
import jax
import jax.numpy as jnp
from jax import lax
import numpy as np

D_MODEL = 2048
BATCH = 4
SEQ = 4096
DEPTH = 1

N_META = 16
BLOCK = 128
PAD = (-N_META) % BLOCK
N_DIFF_HEADS = 8
DIFF_QK_DIM = 64
DIFF_V_DIM = 2 * DIFF_QK_DIM
N_SB_HEADS = 8
SB_HEAD_DIM = 128
DIFF_Q_W = N_DIFF_HEADS * 2 * DIFF_QK_DIM
DIFF_K_W = N_DIFF_HEADS * 2 * DIFF_QK_DIM
DIFF_V_W = N_DIFF_HEADS * DIFF_V_DIM
SB_W = N_SB_HEADS * SB_HEAD_DIM
MIX_W = DIFF_V_W + SB_W
IN_W = DIFF_Q_W + DIFF_K_W + DIFF_V_W + 3 * SB_W
D_FF = 4 * D_MODEL
ROPE_THETA = 10000.0
EPS = 1e-6
NEG_INF = -1e30

kernel_name = "hymba_diff_stickbreaking_hybrid"


def _rmsnorm(x, g):
    xf = x.astype(jnp.float32)
    y = xf * lax.rsqrt(jnp.mean(xf * xf, axis=-1, keepdims=True) + EPS)
    return (y * g.astype(jnp.float32)).astype(x.dtype)


def _rope_tables(length, dim):
    pos = jnp.arange(length, dtype=jnp.float32) - PAD
    inv = ROPE_THETA ** (-jnp.arange(0, dim, 2, dtype=jnp.float32) / dim)
    ang = pos[:, None] * inv[None, :]
    return jnp.cos(ang), jnp.sin(ang)


def _rope(x, cos, sin):
    half = x.shape[-1] // 2
    shape = (1, cos.shape[0]) + (1,) * (x.ndim - 3) + (half,)
    c = cos.reshape(shape).astype(x.dtype)
    s = sin.reshape(shape).astype(x.dtype)
    x1, x2 = x[..., :half], x[..., half:]
    return jnp.concatenate([x1 * c - x2 * s, x2 * c + x1 * s], axis=-1)


def _to_blocks(q):
    b, h, l, d = q.shape
    return q.reshape(b, h, l // BLOCK, BLOCK, d).transpose(2, 0, 1, 3, 4)


def _from_blocks(o):
    nb, b, h, blk, e = o.shape
    return o.transpose(1, 0, 3, 2, 4).reshape(b, nb * blk, h, e)


def _diff_attention(q1, q2, k1, k2, v, lam):
    length, d = q1.shape[2], q1.shape[3]
    kpos = jnp.arange(length)
    scale = d ** -0.5

    def one_block(args):
        i, q1b, q2b = args
        qpos = i * BLOCK + jnp.arange(BLOCK)
        mask = (kpos[None, :] <= qpos[:, None]) & (kpos[None, :] >= PAD)

        def probs(qb, k):
            s = jnp.einsum('bhqd,bhkd->bhqk', qb, k).astype(jnp.float32) * scale
            return jax.nn.softmax(jnp.where(mask, s, NEG_INF), axis=-1)

        w = probs(q1b, k1) - lam * probs(q2b, k2)
        return jnp.einsum('bhqk,bhkv->bhqv', w.astype(v.dtype), v)

    nb = length // BLOCK
    out = lax.map(one_block, (jnp.arange(nb), _to_blocks(q1), _to_blocks(q2)))
    return _from_blocks(out)


def _stick_breaking_attention(q, k, v):
    length, d = q.shape[2], q.shape[3]
    kpos = jnp.arange(length)
    scale = d ** -0.5

    def one_block(args):
        i, qb = args
        qpos = i * BLOCK + jnp.arange(BLOCK)
        strict = (kpos[None, :] < qpos[:, None]) & (kpos[None, :] >= PAD)
        z = jnp.einsum('bhqd,bhkd->bhqk', qb, k).astype(jnp.float32) * scale
        log_not = jnp.where(strict, jax.nn.log_sigmoid(-z), 0.0)
        suffix = lax.cumsum(log_not, axis=3, reverse=True) - log_not
        log_a = jax.nn.log_sigmoid(z) + suffix
        a = jnp.where(strict, jnp.exp(log_a), 0.0)
        return jnp.einsum('bhqk,bhkd->bhqd', a.astype(v.dtype), v)

    nb = length // BLOCK
    out = lax.map(one_block, (jnp.arange(nb), _to_blocks(q)))
    return _from_blocks(out)


def setup_inputs(seed: int = 0) -> dict:
    key = jax.random.key(seed)
    ks = jax.random.split(key, 20)
    f32 = jnp.float32

    def nrm(k, shape, scale):
        return jax.random.normal(k, shape, f32) * scale

    def gain(k, shape):
        return 1.0 + 0.02 * jax.random.normal(k, shape, f32)

    return {
        "x": nrm(ks[0], (BATCH, SEQ, D_MODEL), 1.0),
        "meta_tokens": nrm(ks[1], (N_META, D_MODEL), 1.0),
        "g_mix": gain(ks[2], (DEPTH, D_MODEL)),
        "w_in": nrm(ks[3], (DEPTH, D_MODEL, IN_W), D_MODEL ** -0.5),
        "q_norm_g": gain(ks[4], (DEPTH, DIFF_QK_DIM)),
        "k_norm_g": gain(ks[5], (DEPTH, DIFF_QK_DIM)),
        "lambda_q1": nrm(ks[6], (DEPTH, DIFF_QK_DIM), 0.1),
        "lambda_k1": nrm(ks[7], (DEPTH, DIFF_QK_DIM), 0.1),
        "lambda_q2": nrm(ks[8], (DEPTH, DIFF_QK_DIM), 0.1),
        "lambda_k2": nrm(ks[9], (DEPTH, DIFF_QK_DIM), 0.1),
        "diff_out_g": gain(ks[10], (DEPTH, DIFF_V_DIM)),
        "sb_out_g": gain(ks[11], (DEPTH, SB_HEAD_DIM)),
        "w_out": nrm(ks[12], (DEPTH, MIX_W, D_MODEL), MIX_W ** -0.5),
        "g_mlp": gain(ks[13], (DEPTH, D_MODEL)),
        "w_up": nrm(ks[14], (DEPTH, D_MODEL, D_FF), D_MODEL ** -0.5),
        "w_down": nrm(ks[15], (DEPTH, D_FF, D_MODEL), D_FF ** -0.5),
    }


def reference(x, meta_tokens, g_mix, w_in, q_norm_g, k_norm_g, lambda_q1, lambda_k1,
              lambda_q2, lambda_k2, diff_out_g, sb_out_g, w_out, g_mlp, w_up, w_down):
    b, seq, dm = x.shape
    dummy = jnp.zeros((b, PAD, dm), x.dtype)
    meta = jnp.broadcast_to(meta_tokens.astype(x.dtype)[None], (b, N_META, dm))
    h = jnp.concatenate([dummy, meta, x], axis=1)
    length = h.shape[1]
    cos, sin = _rope_tables(length, DIFF_QK_DIM)
    split_at = list(np.cumsum([DIFF_Q_W, DIFF_K_W, DIFF_V_W, SB_W, SB_W]))

    for l in range(DEPTH):
        lambda_init = 0.8 - 0.6 * math.exp(-0.3 * l)
        u = _rmsnorm(h, g_mix[l])
        proj = jnp.einsum('bld,de->ble', u, w_in[l])
        dq, dk, dv, sq, sk, sv = jnp.split(proj, split_at, axis=-1)

        dq = dq.reshape(b, length, N_DIFF_HEADS, 2, DIFF_QK_DIM)
        dk = dk.reshape(b, length, N_DIFF_HEADS, 2, DIFF_QK_DIM)
        dq = _rope(_rmsnorm(dq, q_norm_g[l]), cos, sin).transpose(0, 2, 3, 1, 4)
        dk = _rope(_rmsnorm(dk, k_norm_g[l]), cos, sin).transpose(0, 2, 3, 1, 4)
        dv = dv.reshape(b, length, N_DIFF_HEADS, DIFF_V_DIM).transpose(0, 2, 1, 3)
        lam = (jnp.exp(jnp.sum(lambda_q1[l] * lambda_k1[l]).astype(jnp.float32))
               - jnp.exp(jnp.sum(lambda_q2[l] * lambda_k2[l]).astype(jnp.float32))
               + lambda_init)
        o_diff = _diff_attention(dq[:, :, 0], dq[:, :, 1], dk[:, :, 0], dk[:, :, 1], dv, lam)
        o_diff = (_rmsnorm(o_diff, diff_out_g[l]) * (1.0 - lambda_init)).reshape(b, length, DIFF_V_W)

        sq = sq.reshape(b, length, N_SB_HEADS, SB_HEAD_DIM).transpose(0, 2, 1, 3)
        sk = sk.reshape(b, length, N_SB_HEADS, SB_HEAD_DIM).transpose(0, 2, 1, 3)
        sv = sv.reshape(b, length, N_SB_HEADS, SB_HEAD_DIM).transpose(0, 2, 1, 3)
        o_sb = _stick_breaking_attention(sq, sk, sv)
        o_sb = _rmsnorm(o_sb, sb_out_g[l]).reshape(b, length, SB_W)

        mixed = jnp.concatenate([o_diff, o_sb], axis=-1)
        h = h + jnp.einsum('ble,ed->bld', mixed, w_out[l])

        m = _rmsnorm(h, g_mlp[l])
        hid = jnp.square(jax.nn.relu(jnp.einsum('bld,df->blf', m, w_up[l])))
        h = h + jnp.einsum('blf,fd->bld', hid, w_down[l])

    return h[:, PAD + N_META:]
```

```python
import functools
import math

import jax
import jax.numpy as jnp
from jax import lax
from jax.experimental import pallas as pl
from jax.experimental.pallas import tpu as pltpu

N_META = 16
N_HEADS = 8
HEAD_W = 128
QK_DIM = 64
N_GROUPS = 6 * N_HEADS
ROPE_THETA = 10000.0
EPS = 1e-6
NEG_INF = -1e30
LAMBDA_INIT = 0.8 - 0.6 * math.exp(0.0)

META_PAD = 128
ATTN_BLOCK = 256
VMEM_LIMIT = 56 * 1024 * 1024

f32 = jnp.float32
bf16 = jnp.bfloat16


def _lane_iota(shape):
    return lax.broadcasted_iota(jnp.int32, shape, len(shape) - 1)


def _inproj_kernel(x_ref, g_ref, w_ref, cos_ref, sin_ref, qg_ref, kg_ref, o_ref, u_ref, *, tn):
    j = pl.program_id(1)

    @pl.when(j == 0)
    def _():
        xf = x_ref[...]
        ms = jnp.mean(xf * xf, axis=-1, keepdims=True)
        u_ref[...] = (xf * lax.rsqrt(ms + EPS) * g_ref[...]).astype(bf16)

    acc = jnp.dot(u_ref[...], w_ref[...], preferred_element_type=f32)
    ngrp = tn // HEAD_W
    grp0 = j * ngrp
    is_dq = grp0 < N_HEADS
    is_dqk = grp0 < 2 * N_HEADS
    is_sq = jnp.logical_and(grp0 >= 3 * N_HEADS, grp0 < 4 * N_HEADS)

    @pl.when(is_dqk)
    def _():
        gain = jnp.where(is_dq, qg_ref[...], kg_ref[...])
        scale = jnp.where(is_dq, QK_DIM ** -0.5, 1.0).astype(f32)
        lane = _lane_iota((1, HEAD_W))
        first = lane < QK_DIM
        low_half = (lane & (QK_DIM // 2)) == 0
        cos = cos_ref[...]
        sin = sin_ref[...]
        for g in range(ngrp):
            t = acc[:, g * HEAD_W:(g + 1) * HEAD_W]
            ss = t * t
            tot = jnp.sum(ss, axis=-1, keepdims=True)
            lo = jnp.sum(jnp.where(first, ss, 0.0), axis=-1, keepdims=True)
            msq = jnp.where(first, lo, tot - lo) * (1.0 / QK_DIM)
            y = t * lax.rsqrt(msq + EPS) * gain
            rot = jnp.where(low_half, pltpu.roll(y, HEAD_W - QK_DIM // 2, 1), pltpu.roll(y, QK_DIM // 2, 1))
            o_ref[0, g] = ((y * cos + rot * sin) * scale).astype(bf16)

    @pl.when(is_sq)
    def _():
        for g in range(ngrp):
            o_ref[0, g] = (acc[:, g * HEAD_W:(g + 1) * HEAD_W] * (HEAD_W ** -0.5)).astype(bf16)

    @pl.when(jnp.logical_not(jnp.logical_or(is_dqk, is_sq)))
    def _():
        for g in range(ngrp):
            o_ref[0, g] = acc[:, g * HEAD_W:(g + 1) * HEAD_W].astype(bf16)


def _inproj(x2d, g_mix, w_in, cos_t, sin_t, qg, kg, *, batch, seq, tm, tn):
    rows, d = x2d.shape
    nb = seq // tm
    ngrp = tn // HEAD_W
    return pl.pallas_call(
        functools.partial(_inproj_kernel, tn=tn),
        grid=(rows // tm, w_in.shape[1] // tn),
        in_specs=[
            pl.BlockSpec((tm, d), lambda i, j: (i, 0)),
            pl.BlockSpec((1, d), lambda i, j: (0, 0)),
            pl.BlockSpec((d, tn), lambda i, j: (0, j)),
            pl.BlockSpec((tm, HEAD_W), lambda i, j: (i % nb, 0)),
            pl.BlockSpec((tm, HEAD_W), lambda i, j: (i % nb, 0)),
            pl.BlockSpec((1, HEAD_W), lambda i, j: (0, 0)),
            pl.BlockSpec((1, HEAD_W), lambda i, j: (0, 0)),
        ],
        out_specs=pl.BlockSpec((1, ngrp, tm, HEAD_W), lambda i, j: (i // nb, j, i % nb, 0)),
        out_shape=jax.ShapeDtypeStruct((batch, N_GROUPS, seq, HEAD_W), bf16),
        scratch_shapes=[pltpu.VMEM((tm, d), bf16)],
        compiler_params=pltpu.CompilerParams(
            dimension_semantics=("parallel", "arbitrary"), vmem_limit_bytes=VMEM_LIMIT),
        name="inproj",
    )(x2d, g_mix, w_in, cos_t, sin_t, qg, kg)


def _diff_kernel(lq1_ref, lk1_ref, lq2_ref, lk2_ref, og_ref, q_ref, k_ref, v_ref, km_ref, vm_ref, o_ref, *, t):
    qi = pl.program_id(2)
    q = q_ref[0, 0]
    lane_q = _lane_iota((t, HEAD_W))
    zero = jnp.zeros_like(q)
    qs = jnp.concatenate([jnp.where(lane_q < QK_DIM, q, zero), jnp.where(lane_q >= QK_DIM, q, zero)], axis=0)

    lam = (jnp.exp(jnp.sum(lq1_ref[...] * lk1_ref[...], axis=-1, keepdims=True))
           - jnp.exp(jnp.sum(lq2_ref[...] * lk2_ref[...], axis=-1, keepdims=True))
           + LAMBDA_INIT)

    def step(kblk, vblk, mask, carry):
        m, l, acc = carry
        s = lax.dot_general(qs, kblk, (((1,), (1,)), ((), ())), preferred_element_type=f32)
        if mask is not None:
            s = jnp.where(mask, s, NEG_INF)
        m_new = jnp.maximum(m, jnp.max(s, axis=-1, keepdims=True))
        alpha = jnp.exp(m - m_new)
        p = jnp.exp(s - m_new)
        l = alpha * l + jnp.sum(p, axis=-1, keepdims=True)
        acc = alpha * acc + jnp.dot(p.astype(bf16), vblk, preferred_element_type=f32)
        return m_new, l, acc

    carry = (jnp.full((2 * t, 1), NEG_INF, f32), jnp.zeros((2 * t, 1), f32), jnp.zeros((2 * t, HEAD_W), f32))
    carry = step(km_ref[0, 0], vm_ref[0, 0], _lane_iota((2 * t, META_PAD)) < N_META, carry)

    def full_block(kb, c):
        start = pl.multiple_of(kb * t, t)
        return step(k_ref[0, 0, pl.ds(start, t), :], v_ref[0, 0, pl.ds(start, t), :], None, c)

    carry = lax.fori_loop(0, qi, full_block, carry)

    row = lax.broadcasted_iota(jnp.int32, (2 * t, t), 0)
    col = lax.broadcasted_iota(jnp.int32, (2 * t, t), 1)
    qrow = jnp.where(row >= t, row - t, row)
    start = pl.multiple_of(qi * t, t)
    m, l, acc = step(k_ref[0, 0, pl.ds(start, t), :], v_ref[0, 0, pl.ds(start, t), :], col <= qrow, carry)

    o = acc / l
    o = o[:t] - lam * o[t:]
    ms = jnp.mean(o * o, axis=-1, keepdims=True)
    o_ref[...] = (o * lax.rsqrt(ms + EPS) * og_ref[...] * (1.0 - LAMBDA_INIT)).astype(bf16)


def _diff_attention(proj, proj_meta, lq1, lk1, lq2, lk2, og, *, t):
    batch, _, seq, _ = proj.shape
    nq = seq // t
    vec = lambda n: pl.BlockSpec((1, n), lambda b, h, i: (0, 0))
    return pl.pallas_call(
        functools.partial(_diff_kernel, t=t),
        grid=(batch, N_HEADS, nq),
        in_specs=[
            vec(QK_DIM), vec(QK_DIM), vec(QK_DIM), vec(QK_DIM), vec(HEAD_W),
            pl.BlockSpec((1, 1, t, HEAD_W), lambda b, h, i: (b, h, i, 0)),
            pl.BlockSpec((1, 1, seq, HEAD_W), lambda b, h, i: (b, N_HEADS + h, 0, 0)),
            pl.BlockSpec((1, 1, seq, HEAD_W), lambda b, h, i: (b, 2 * N_HEADS + h, 0, 0)),
            pl.BlockSpec((1, 1, META_PAD, HEAD_W), lambda b, h, i: (0, N_HEADS + h, 0, 0)),
            pl.BlockSpec((1, 1, META_PAD, HEAD_W), lambda b, h, i: (0, 2 * N_HEADS + h, 0, 0)),
        ],
        out_specs=pl.BlockSpec((t, HEAD_W), lambda b, h, i: (b * nq + i, h)),
        out_shape=jax.ShapeDtypeStruct((batch * seq, N_HEADS * HEAD_W), bf16),
        compiler_params=pltpu.CompilerParams(
            dimension_semantics=("parallel", "parallel", "arbitrary"), vmem_limit_bytes=VMEM_LIMIT),
        name="diff_attn",
    )(lq1, lk1, lq2, lk2, og, proj, proj, proj, proj_meta, proj_meta)


def _sb_kernel(og_ref, q_ref, k_ref, v_ref, km_ref, vm_ref, o_ref, *, t):
    qi = pl.program_id(2)
    q = q_ref[0, 0]

    def later_key_matrix(n):
        r = lax.broadcasted_iota(jnp.int32, (n, n), 0)
        c = lax.broadcasted_iota(jnp.int32, (n, n), 1)
        return (r > c).astype(bf16)

    def step(kblk, vblk, u, mask, carry):
        run, acc = carry
        z = lax.dot_general(q, kblk, (((1,), (1,)), ((), ())), preferred_element_type=f32)
        sp = jnp.maximum(z, 0.0) + jnp.log1p(jnp.exp(-jnp.abs(z)))
        log_not = -sp
        if mask is not None:
            log_not = jnp.where(mask, log_not, 0.0)
        hi = log_not.astype(bf16)
        lo = (log_not - hi.astype(f32)).astype(bf16)
        suffix = (jnp.dot(hi, u, preferred_element_type=f32) + jnp.dot(lo, u, preferred_element_type=f32))
        a = jnp.exp(z - sp + suffix + run)
        if mask is not None:
            a = jnp.where(mask, a, 0.0)
        acc = acc + jnp.dot(a.astype(bf16), vblk, preferred_element_type=f32)
        run = run + jnp.sum(log_not, axis=-1, keepdims=True)
        return run, acc

    u_t = later_key_matrix(t)
    carry = (jnp.zeros((t, 1), f32), jnp.zeros((t, HEAD_W), f32))

    row = lax.broadcasted_iota(jnp.int32, (t, t), 0)
    col = lax.broadcasted_iota(jnp.int32, (t, t), 1)
    start = pl.multiple_of(qi * t, t)
    carry = step(k_ref[0, 0, pl.ds(start, t), :], v_ref[0, 0, pl.ds(start, t), :], u_t, col < row, carry)

    def full_block(n, c):
        start = pl.multiple_of((qi - 1 - n) * t, t)
        return step(k_ref[0, 0, pl.ds(start, t), :], v_ref[0, 0, pl.ds(start, t), :], u_t, None, c)

    carry = lax.fori_loop(0, qi, full_block, carry)

    _, acc = step(km_ref[0, 0], vm_ref[0, 0], later_key_matrix(META_PAD),
                  _lane_iota((t, META_PAD)) < N_META, carry)

    ms = jnp.mean(acc * acc, axis=-1, keepdims=True)
    o_ref[...] = (acc * lax.rsqrt(ms + EPS) * og_ref[...]).astype(bf16)


def _sb_attention(proj, proj_meta, og, *, t):
    batch, _, seq, _ = proj.shape
    nq = seq // t
    return pl.pallas_call(
        functools.partial(_sb_kernel, t=t),
        grid=(batch, N_HEADS, nq),
        in_specs=[
            pl.BlockSpec((1, HEAD_W), lambda b, h, i: (0, 0)),
            pl.BlockSpec((1, 1, t, HEAD_W), lambda b, h, i: (b, 3 * N_HEADS + h, i, 0)),
            pl.BlockSpec((1, 1, seq, HEAD_W), lambda b, h, i: (b, 4 * N_HEADS + h, 0, 0)),
            pl.BlockSpec((1, 1, seq, HEAD_W), lambda b, h, i: (b, 5 * N_HEADS + h, 0, 0)),
            pl.BlockSpec((1, 1, META_PAD, HEAD_W), lambda b, h, i: (0, 4 * N_HEADS + h, 0, 0)),
            pl.BlockSpec((1, 1, META_PAD, HEAD_W), lambda b, h, i: (0, 5 * N_HEADS + h, 0, 0)),
        ],
        out_specs=pl.BlockSpec((t, HEAD_W), lambda b, h, i: (b * nq + i, h)),
        out_shape=jax.ShapeDtypeStruct((batch * seq, N_HEADS * HEAD_W), bf16),
        compiler_params=pltpu.CompilerParams(
            dimension_semantics=("parallel", "parallel", "arbitrary"), vmem_limit_bytes=VMEM_LIMIT),
        name="sb_attn",
    )(og, proj, proj, proj, proj_meta, proj_meta)


def _outproj_kernel(x_ref, md_ref, ms_ref, wd_ref, ws_ref, o_ref):
    o_ref[...] = (x_ref[...]
                  + jnp.dot(md_ref[...], wd_ref[...], preferred_element_type=f32)
                  + jnp.dot(ms_ref[...], ws_ref[...], preferred_element_type=f32))


def _outproj(x2d, mixed_diff, mixed_sb, w_out, *, tm):
    rows, d = x2d.shape
    half = mixed_diff.shape[1]
    return pl.pallas_call(
        _outproj_kernel,
        grid=(rows // tm,),
        in_specs=[
            pl.BlockSpec((tm, d), lambda i: (i, 0)),
            pl.BlockSpec((tm, half), lambda i: (i, 0)),
            pl.BlockSpec((tm, half), lambda i: (i, 0)),
            pl.BlockSpec((half, d), lambda i: (0, 0)),
            pl.BlockSpec((half, d), lambda i: (1, 0)),
        ],
        out_specs=pl.BlockSpec((tm, d), lambda i: (i, 0)),
        out_shape=jax.ShapeDtypeStruct((rows, d), f32),
        compiler_params=pltpu.CompilerParams(
            dimension_semantics=("parallel",), vmem_limit_bytes=VMEM_LIMIT),
        name="outproj",
    )(x2d, mixed_diff, mixed_sb, w_out, w_out)


def _mlp_kernel(h_ref, g_ref, wu_ref, wd_ref, o_ref, m_ref):
    @pl.when(pl.program_id(1) == 0)
    def _():
        h = h_ref[...]
        ms = jnp.mean(h * h, axis=-1, keepdims=True)
        m_ref[...] = (h * lax.rsqrt(ms + EPS) * g_ref[...]).astype(bf16)
        o_ref[...] = h

    hid = jnp.dot(m_ref[...], wu_ref[...], preferred_element_type=f32)
    hid = jnp.square(jnp.maximum(hid, 0.0))
    o_ref[...] += jnp.dot(hid.astype(bf16), wd_ref[...], preferred_element_type=f32)


def _mlp(h1, g_mlp, w_up, w_down, *, tm, tf):
    rows, d = h1.shape
    return pl.pallas_call(
        _mlp_kernel,
        grid=(rows // tm, w_up.shape[1] // tf),
        in_specs=[
            pl.BlockSpec((tm, d), lambda i, f: (i, 0)),
            pl.BlockSpec((1, d), lambda i, f: (0, 0)),
            pl.BlockSpec((d, tf), lambda i, f: (0, f)),
            pl.BlockSpec((tf, d), lambda i, f: (f, 0)),
        ],
        out_specs=pl.BlockSpec((tm, d), lambda i, f: (i, 0)),
        out_shape=jax.ShapeDtypeStruct((rows, d), f32),
        scratch_shapes=[pltpu.VMEM((tm, d), bf16)],
        compiler_params=pltpu.CompilerParams(
            dimension_semantics=("parallel", "arbitrary"), vmem_limit_bytes=VMEM_LIMIT),
        name="mlp",
    )(h1, g_mlp, w_up, w_down)


def _rope_tables(n_pos):
    pos = jnp.arange(n_pos, dtype=f32)
    inv = ROPE_THETA ** (-jnp.arange(0, QK_DIM, 2, dtype=f32) / QK_DIM)
    ang = pos[:, None] * inv[None, :]
    cos, sin = jnp.cos(ang), jnp.sin(ang)
    return jnp.concatenate([cos, cos, cos, cos], axis=-1), jnp.concatenate([-sin, sin, -sin, sin], axis=-1)


def kernel(x, meta_tokens, g_mix, w_in, q_norm_g, k_norm_g, lambda_q1, lambda_k1, lambda_q2, lambda_k2,
           diff_out_g, sb_out_g, w_out, g_mlp, w_up, w_down):
    batch, seq, d = x.shape
    assert g_mix.shape[0] == 1, "single-layer kernel"
    assert meta_tokens.shape[0] == N_META and seq % ATTN_BLOCK == 0

    x2d = x.reshape(batch * seq, d)
    w_in_b = w_in[0].astype(bf16)
    w_out_b = w_out[0].astype(bf16)
    w_up_b = w_up[0].astype(bf16)
    w_down_b = w_down[0].astype(bf16)
    qg = jnp.tile(q_norm_g[0], 2)[None, :]
    kg = jnp.tile(k_norm_g[0], 2)[None, :]
    cos_t, sin_t = _rope_tables(N_META + seq)

    proj = _inproj(x2d, g_mix, w_in_b, cos_t[N_META:], sin_t[N_META:], qg, kg,
                   batch=batch, seq=seq, tm=512, tn=512)
    proj_meta = _inproj(meta_tokens.astype(f32), g_mix, w_in_b, cos_t[:N_META], sin_t[:N_META], qg, kg,
                        batch=1, seq=N_META, tm=N_META, tn=512)
    proj_meta = jnp.pad(proj_meta, ((0, 0), (0, 0), (0, META_PAD - N_META), (0, 0)))

    mixed_diff = _diff_attention(proj, proj_meta, lambda_q1, lambda_k1, lambda_q2, lambda_k2, diff_out_g,
                                 t=ATTN_BLOCK)
    mixed_sb = _sb_attention(proj, proj_meta, sb_out_g, t=ATTN_BLOCK)

    h1 = _outproj(x2d, mixed_diff, mixed_sb, w_out_b, tm=512)
    out = _mlp(h1, g_mlp, w_up_b, w_down_b, tm=512, tf=512)
    return out.reshape(batch, seq, d)
```

```python
import functools
import math

import jax
import jax.numpy as jnp
from jax import lax
from jax.experimental import pallas as pl
from jax.experimental.pallas import tpu as pltpu

N_META = 16
N_HEADS = 8
HEAD_W = 128
QK_DIM = 64
N_GROUPS = 6 * N_HEADS
ROPE_THETA = 10000.0
EPS = 1e-6
NEG_INF = -1e30
LAMBDA_INIT = 0.8 - 0.6 * math.exp(0.0)
LOG2E = math.log2(math.e)

META_PAD = 128
ATTN_BLOCK = 256
DIFF_HEADS_PER_STEP = 2
SB_HEADS_PER_STEP = 4
DIFF_WIDE = 2
VMEM_LIMIT = 56 * 1024 * 1024

f32 = jnp.float32
bf16 = jnp.bfloat16


def _lane_iota(shape):
    return lax.broadcasted_iota(jnp.int32, shape, len(shape) - 1)


def _inproj_kernel(x_ref, g_ref, w_ref, cos_ref, sin_ref, qg_ref, kg_ref, o_ref, u_ref, *, tn):
    j = pl.program_id(1)

    @pl.when(j == 0)
    def _():
        xf = x_ref[...]
        ms = jnp.mean(xf * xf, axis=-1, keepdims=True)
        u_ref[...] = (xf * lax.rsqrt(ms + EPS) * g_ref[...]).astype(bf16)

    acc = jnp.dot(u_ref[...], w_ref[...], preferred_element_type=f32)
    ngrp = tn // HEAD_W
    grp0 = j * ngrp
    is_dq = grp0 < N_HEADS
    is_dqk = grp0 < 2 * N_HEADS
    is_sq = jnp.logical_and(grp0 >= 3 * N_HEADS, grp0 < 4 * N_HEADS)

    @pl.when(is_dqk)
    def _():
        gain = jnp.where(is_dq, qg_ref[...], kg_ref[...])
        scale = jnp.where(is_dq, QK_DIM ** -0.5 * LOG2E, 1.0).astype(f32)
        lane = _lane_iota((1, HEAD_W))
        first = lane < QK_DIM
        low_half = (lane & (QK_DIM // 2)) == 0
        cos = cos_ref[...]
        sin = sin_ref[...]
        for g in range(ngrp):
            t = acc[:, g * HEAD_W:(g + 1) * HEAD_W]
            ss = t * t
            tot = jnp.sum(ss, axis=-1, keepdims=True)
            lo = jnp.sum(jnp.where(first, ss, 0.0), axis=-1, keepdims=True)
            msq = jnp.where(first, lo, tot - lo) * (1.0 / QK_DIM)
            y = t * lax.rsqrt(msq + EPS) * gain
            rot = jnp.where(low_half, pltpu.roll(y, HEAD_W - QK_DIM // 2, 1), pltpu.roll(y, QK_DIM // 2, 1))
            o_ref[0, g] = ((y * cos + rot * sin) * scale).astype(bf16)

    @pl.when(is_sq)
    def _():
        for g in range(ngrp):
            o_ref[0, g] = (acc[:, g * HEAD_W:(g + 1) * HEAD_W] * (HEAD_W ** -0.5)).astype(bf16)

    @pl.when(jnp.logical_not(jnp.logical_or(is_dqk, is_sq)))
    def _():
        for g in range(ngrp):
            o_ref[0, g] = acc[:, g * HEAD_W:(g + 1) * HEAD_W].astype(bf16)


def _inproj(x2d, g_mix, w_in, cos_t, sin_t, qg, kg, *, batch, seq, tm, tn):
    rows, d = x2d.shape
    nb = seq // tm
    ngrp = tn // HEAD_W
    return pl.pallas_call(
        functools.partial(_inproj_kernel, tn=tn),
        grid=(rows // tm, w_in.shape[1] // tn),
        in_specs=[
            pl.BlockSpec((tm, d), lambda i, j: (i, 0)),
            pl.BlockSpec((1, d), lambda i, j: (0, 0)),
            pl.BlockSpec((d, tn), lambda i, j: (0, j)),
            pl.BlockSpec((tm, HEAD_W), lambda i, j: (i % nb, 0)),
            pl.BlockSpec((tm, HEAD_W), lambda i, j: (i % nb, 0)),
            pl.BlockSpec((1, HEAD_W), lambda i, j: (0, 0)),
            pl.BlockSpec((1, HEAD_W), lambda i, j: (0, 0)),
        ],
        out_specs=pl.BlockSpec((1, ngrp, tm, HEAD_W), lambda i, j: (i // nb, j, i % nb, 0)),
        out_shape=jax.ShapeDtypeStruct((batch, N_GROUPS, seq, HEAD_W), bf16),
        scratch_shapes=[pltpu.VMEM((tm, d), bf16)],
        compiler_params=pltpu.CompilerParams(
            dimension_semantics=("parallel", "arbitrary"), vmem_limit_bytes=VMEM_LIMIT),
        name="inproj",
    )(x2d, g_mix, w_in, cos_t, sin_t, qg, kg)


def _attn_specs(seq, t, hps, q_grp, k_grp, v_grp):
    return [
        pl.BlockSpec((1, hps, t, HEAD_W), lambda b, h, i: (b, q_grp // hps + h, i, 0)),
        pl.BlockSpec((1, hps, seq, HEAD_W), lambda b, h, i: (b, k_grp // hps + h, 0, 0)),
        pl.BlockSpec((1, hps, seq, HEAD_W), lambda b, h, i: (b, v_grp // hps + h, 0, 0)),
        pl.BlockSpec((1, hps, META_PAD, HEAD_W), lambda b, h, i: (0, k_grp // hps + h, 0, 0)),
        pl.BlockSpec((1, hps, META_PAD, HEAD_W), lambda b, h, i: (0, v_grp // hps + h, 0, 0)),
    ]


def _attn_call(kernel_fn, name, small_specs, small_args, proj, proj_meta, grp_specs, *, t, hps):
    batch, _, seq, _ = proj.shape
    nq = seq // t
    return pl.pallas_call(
        functools.partial(kernel_fn, t=t, hps=hps),
        grid=(batch, N_HEADS // hps, nq),
        in_specs=small_specs + grp_specs,
        out_specs=pl.BlockSpec((t, hps * HEAD_W), lambda b, h, i: (b * nq + i, h)),
        out_shape=jax.ShapeDtypeStruct((batch * seq, N_HEADS * HEAD_W), bf16),
        compiler_params=pltpu.CompilerParams(
            dimension_semantics=("parallel", "parallel", "arbitrary"), vmem_limit_bytes=VMEM_LIMIT),
        name=name,
    )(*small_args, proj, proj, proj, proj_meta, proj_meta)


def _diff_kernel(lq1_ref, lk1_ref, lq2_ref, lk2_ref, og_ref, q_ref, k_ref, v_ref, km_ref, vm_ref, o_ref, *, t, hps):
    qi = pl.program_id(2)
    heads = range(hps)
    lane_q = _lane_iota((t, HEAD_W))

    def stacked_q(h):
        q = q_ref[0, h]
        zero = jnp.zeros_like(q)
        return jnp.concatenate([jnp.where(lane_q < QK_DIM, q, zero), jnp.where(lane_q >= QK_DIM, q, zero)], axis=0)

    qs = [stacked_q(h) for h in heads]

    lam = (jnp.exp(jnp.sum(lq1_ref[...] * lk1_ref[...], axis=-1, keepdims=True))
           - jnp.exp(jnp.sum(lq2_ref[...] * lk2_ref[...], axis=-1, keepdims=True))
           + LAMBDA_INIT)

    def steps(kblks, vblks, mask, carries):
        ss = [lax.dot_general(qs[h], kblks[h], (((1,), (1,)), ((), ())), preferred_element_type=f32) for h in heads]
        if mask is not None:
            ss = [jnp.where(mask, s, NEG_INF) for s in ss]
        mid = []
        for h in heads:
            m, l, acc = carries[h]
            m_new = jnp.maximum(m, jnp.max(ss[h], axis=-1, keepdims=True))
            alpha = jnp.exp2(m - m_new)
            p = jnp.exp2(ss[h] - m_new)
            mid.append((m_new, alpha, alpha * l + jnp.sum(p, axis=-1, keepdims=True), p.astype(bf16)))
        return tuple((m_new, l, alpha * carries[h][2] + jnp.dot(p, vblks[h], preferred_element_type=f32))
                     for h, (m_new, alpha, l, p) in zip(heads, mid))

    def blocks_at(start, width):
        return ([k_ref[0, h, pl.ds(start, width), :] for h in heads],
                [v_ref[0, h, pl.ds(start, width), :] for h in heads])

    init = (jnp.full((2 * t, 1), NEG_INF, f32), jnp.zeros((2 * t, 1), f32), jnp.zeros((2 * t, HEAD_W), f32))
    meta_mask = _lane_iota((2 * t, META_PAD)) < N_META
    carry = steps([km_ref[0, h] for h in heads], [vm_ref[0, h] for h in heads], meta_mask, [init for _ in heads])

    wide = DIFF_WIDE * t
    n_wide = qi // DIFF_WIDE

    def wide_block(kb, c):
        return steps(*blocks_at(pl.multiple_of(kb * wide, wide), wide), None, c)

    def full_block(kb, c):
        return steps(*blocks_at(pl.multiple_of(kb * t, t), t), None, c)

    carry = lax.fori_loop(0, n_wide, wide_block, carry)
    carry = lax.fori_loop(n_wide * DIFF_WIDE, qi, full_block, carry)

    row = lax.broadcasted_iota(jnp.int32, (2 * t, t), 0)
    col = lax.broadcasted_iota(jnp.int32, (2 * t, t), 1)
    diag_mask = col <= jnp.where(row >= t, row - t, row)
    carry = steps(*blocks_at(pl.multiple_of(qi * t, t), t), diag_mask, carry)
    for h in heads:
        _, l, acc = carry[h]
        o = acc / l
        o = o[:t] - lam * o[t:]
        ms = jnp.mean(o * o, axis=-1, keepdims=True)
        o_ref[:, h * HEAD_W:(h + 1) * HEAD_W] = (
            o * lax.rsqrt(ms + EPS) * og_ref[...] * (1.0 - LAMBDA_INIT)).astype(bf16)


def _diff_attention(proj, proj_meta, lq1, lk1, lq2, lk2, og, *, t):
    vec = lambda n: pl.BlockSpec((1, n), lambda b, h, i: (0, 0))
    return _attn_call(
        _diff_kernel, "diff_attn",
        [vec(QK_DIM), vec(QK_DIM), vec(QK_DIM), vec(QK_DIM), vec(HEAD_W)], (lq1, lk1, lq2, lk2, og),
        proj, proj_meta, _attn_specs(proj.shape[2], t, DIFF_HEADS_PER_STEP, 0, N_HEADS, 2 * N_HEADS),
        t=t, hps=DIFF_HEADS_PER_STEP)


def _sb_kernel(og_ref, q_ref, k_ref, v_ref, km_ref, vm_ref, o_ref, *, t, hps):
    qi = pl.program_id(2)
    heads = range(hps)
    qh = [q_ref[0, h] for h in heads]

    def later_key_matrix(n):
        r = lax.broadcasted_iota(jnp.int32, (n, n), 0)
        c = lax.broadcasted_iota(jnp.int32, (n, n), 1)
        return (r > c).astype(bf16)

    def steps(kblks, vblks, u, mask, carries):
        zs = [lax.dot_general(qh[h], kblks[h], (((1,), (1,)), ((), ())), preferred_element_type=f32) for h in heads]
        sps, laters = [], []
        for h in heads:
            sp = jnp.maximum(zs[h], 0.0) + jnp.log(1.0 + jnp.exp(-jnp.abs(zs[h])))
            if mask is not None:
                sp = jnp.where(mask, sp, 0.0)
            hi = sp.astype(bf16)
            lo = (sp - hi.astype(f32)).astype(bf16)
            sps.append(sp)
            laters.append(jnp.dot(hi, u, preferred_element_type=f32) + jnp.dot(lo, u, preferred_element_type=f32))
        out = []
        for h in heads:
            run, acc = carries[h]
            a = jnp.exp(zs[h] - sps[h] - laters[h] - run)
            if mask is not None:
                a = jnp.where(mask, a, 0.0)
            out.append((run + jnp.sum(sps[h], axis=-1, keepdims=True),
                        acc + jnp.dot(a.astype(bf16), vblks[h], preferred_element_type=f32)))
        return tuple(out)

    def blocks_at(start):
        return ([k_ref[0, h, pl.ds(start, t), :] for h in heads], [v_ref[0, h, pl.ds(start, t), :] for h in heads])

    u_t = later_key_matrix(t)
    init = (jnp.zeros((t, 1), f32), jnp.zeros((t, HEAD_W), f32))

    row = lax.broadcasted_iota(jnp.int32, (t, t), 0)
    col = lax.broadcasted_iota(jnp.int32, (t, t), 1)
    carry = steps(*blocks_at(pl.multiple_of(qi * t, t)), u_t, col < row, [init for _ in heads])

    def full_block(n, c):
        return steps(*blocks_at(pl.multiple_of((qi - 1 - n) * t, t)), u_t, None, c)

    carry = lax.fori_loop(0, qi, full_block, carry)

    carry = steps([km_ref[0, h] for h in heads], [vm_ref[0, h] for h in heads], later_key_matrix(META_PAD),
                  _lane_iota((t, META_PAD)) < N_META, carry)
    for h in heads:
        acc = carry[h][1]
        ms = jnp.mean(acc * acc, axis=-1, keepdims=True)
        o_ref[:, h * HEAD_W:(h + 1) * HEAD_W] = (acc * lax.rsqrt(ms + EPS) * og_ref[...]).astype(bf16)


def _sb_attention(proj, proj_meta, og, *, t):
    return _attn_call(
        _sb_kernel, "sb_attn", [pl.BlockSpec((1, HEAD_W), lambda b, h, i: (0, 0))], (og,),
        proj, proj_meta, _attn_specs(proj.shape[2], t, SB_HEADS_PER_STEP, 3 * N_HEADS, 4 * N_HEADS, 5 * N_HEADS),
        t=t, hps=SB_HEADS_PER_STEP)


def _outproj_kernel(x_ref, md_ref, ms_ref, wd_ref, ws_ref, o_ref):
    o_ref[...] = (x_ref[...]
                  + jnp.dot(md_ref[...], wd_ref[...], preferred_element_type=f32)
                  + jnp.dot(ms_ref[...], ws_ref[...], preferred_element_type=f32))


def _outproj(x2d, mixed_diff, mixed_sb, w_out, *, tm):
    rows, d = x2d.shape
    half = mixed_diff.shape[1]
    return pl.pallas_call(
        _outproj_kernel,
        grid=(rows // tm,),
        in_specs=[
            pl.BlockSpec((tm, d), lambda i: (i, 0)),
            pl.BlockSpec((tm, half), lambda i: (i, 0)),
            pl.BlockSpec((tm, half), lambda i: (i, 0)),
            pl.BlockSpec((half, d), lambda i: (0, 0)),
            pl.BlockSpec((half, d), lambda i: (1, 0)),
        ],
        out_specs=pl.BlockSpec((tm, d), lambda i: (i, 0)),
        out_shape=jax.ShapeDtypeStruct((rows, d), f32),
        compiler_params=pltpu.CompilerParams(
            dimension_semantics=("parallel",), vmem_limit_bytes=VMEM_LIMIT),
        name="outproj",
    )(x2d, mixed_diff, mixed_sb, w_out, w_out)


def _mlp_kernel(h_ref, g_ref, wu_ref, wd_ref, o_ref, m_ref):
    @pl.when(pl.program_id(1) == 0)
    def _():
        h = h_ref[...]
        ms = jnp.mean(h * h, axis=-1, keepdims=True)
        m_ref[...] = (h * lax.rsqrt(ms + EPS) * g_ref[...]).astype(bf16)
        o_ref[...] = h

    hid = jnp.dot(m_ref[...], wu_ref[...], preferred_element_type=f32)
    hid = jnp.square(jnp.maximum(hid, 0.0))
    o_ref[...] += jnp.dot(hid.astype(bf16), wd_ref[...], preferred_element_type=f32)


def _mlp(h1, g_mlp, w_up, w_down, *, tm, tf):
    rows, d = h1.shape
    return pl.pallas_call(
        _mlp_kernel,
        grid=(rows // tm, w_up.shape[1] // tf),
        in_specs=[
            pl.BlockSpec((tm, d), lambda i, f: (i, 0)),
            pl.BlockSpec((1, d), lambda i, f: (0, 0)),
            pl.BlockSpec((d, tf), lambda i, f: (0, f)),
            pl.BlockSpec((tf, d), lambda i, f: (f, 0)),
        ],
        out_specs=pl.BlockSpec((tm, d), lambda i, f: (i, 0)),
        out_shape=jax.ShapeDtypeStruct((rows, d), f32),
        scratch_shapes=[pltpu.VMEM((tm, d), bf16)],
        compiler_params=pltpu.CompilerParams(
            dimension_semantics=("parallel", "arbitrary"), vmem_limit_bytes=VMEM_LIMIT),
        name="mlp",
    )(h1, g_mlp, w_up, w_down)


def _rope_tables(n_pos):
    pos = jnp.arange(n_pos, dtype=f32)
    inv = ROPE_THETA ** (-jnp.arange(0, QK_DIM, 2, dtype=f32) / QK_DIM)
    ang = pos[:, None] * inv[None, :]
    cos, sin = jnp.cos(ang), jnp.sin(ang)
    return jnp.concatenate([cos, cos, cos, cos], axis=-1), jnp.concatenate([-sin, sin, -sin, sin], axis=-1)


def kernel(x, meta_tokens, g_mix, w_in, q_norm_g, k_norm_g, lambda_q1, lambda_k1, lambda_q2, lambda_k2,
           diff_out_g, sb_out_g, w_out, g_mlp, w_up, w_down):
    batch, seq, d = x.shape
    assert g_mix.shape[0] == 1, "single-layer kernel"
    assert meta_tokens.shape[0] == N_META and seq % ATTN_BLOCK == 0

    x2d = x.reshape(batch * seq, d)
    w_in_b = w_in[0].astype(bf16)
    w_out_b = w_out[0].astype(bf16)
    w_up_b = w_up[0].astype(bf16)
    w_down_b = w_down[0].astype(bf16)
    qg = jnp.tile(q_norm_g[0], 2)[None, :]
    kg = jnp.tile(k_norm_g[0], 2)[None, :]
    cos_t, sin_t = _rope_tables(N_META + seq)

    proj = _inproj(x2d, g_mix, w_in_b, cos_t[N_META:], sin_t[N_META:], qg, kg,
                   batch=batch, seq=seq, tm=512, tn=512)
    proj_meta = _inproj(meta_tokens.astype(f32), g_mix, w_in_b, cos_t[:N_META], sin_t[:N_META], qg, kg,
                        batch=1, seq=N_META, tm=N_META, tn=512)
    proj_meta = jnp.pad(proj_meta, ((0, 0), (0, 0), (0, META_PAD - N_META), (0, 0)))

    mixed_diff = _diff_attention(proj, proj_meta, lambda_q1, lambda_k1, lambda_q2, lambda_k2, diff_out_g,
                                 t=ATTN_BLOCK)
    mixed_sb = _sb_attention(proj, proj_meta, sb_out_g, t=ATTN_BLOCK)

    h1 = _outproj(x2d, mixed_diff, mixed_sb, w_out_b, tm=512)
    out = _mlp(h1, g_mlp, w_up_b, w_down_b, tm=512, tf=512)
    return out.reshape(batch, seq, d)
```

```python
import functools
import math

import jax
import jax.numpy as jnp
from jax import lax
from jax.experimental import pallas as pl
from jax.experimental.pallas import tpu as pltpu

N_META = 16
N_HEADS = 8
HEAD_W = 128
QK_DIM = 64
N_GROUPS = 6 * N_HEADS
ROPE_THETA = 10000.0
EPS = 1e-6
NEG_INF = -1e30
LAMBDA_INIT = 0.8 - 0.6 * math.exp(0.0)
LOG2E = math.log2(math.e)

META_PAD = 128
ATTN_BLOCK = 256
DIFF_HEADS_PER_STEP = 2
SB_HEADS_PER_STEP = 4
VMEM_LIMIT = 56 * 1024 * 1024

f32 = jnp.float32
bf16 = jnp.bfloat16


def _lane_iota(shape):
    return lax.broadcasted_iota(jnp.int32, shape, len(shape) - 1)


def _inproj_kernel(x_ref, g_ref, w_ref, cos_ref, sin_ref, qg_ref, kg_ref, o_ref, *, tn):
    xf = x_ref[...]
    ms = jnp.mean(xf * xf, axis=-1, keepdims=True)
    u = (xf * lax.rsqrt(ms + EPS) * g_ref[...]).astype(bf16)

    ngrp = tn // HEAD_W
    lane = _lane_iota((1, HEAD_W))
    first = lane < QK_DIM
    low_half = (lane & (QK_DIM // 2)) == 0
    cos = cos_ref[...]
    sin = sin_ref[...]

    for j in range(w_ref.shape[1] // tn):
        acc = jnp.dot(u, w_ref[:, j * tn:(j + 1) * tn], preferred_element_type=f32)
        for g in range(ngrp):
            grp = j * ngrp + g
            t = acc[:, g * HEAD_W:(g + 1) * HEAD_W]
            if grp < 2 * N_HEADS:
                is_q = grp < N_HEADS
                ss = t * t
                tot = jnp.sum(ss, axis=-1, keepdims=True)
                lo = jnp.sum(jnp.where(first, ss, 0.0), axis=-1, keepdims=True)
                msq = jnp.where(first, lo, tot - lo) * (1.0 / QK_DIM)
                y = t * lax.rsqrt(msq + EPS) * (qg_ref[...] if is_q else kg_ref[...])
                rot = jnp.where(low_half, pltpu.roll(y, HEAD_W - QK_DIM // 2, 1), pltpu.roll(y, QK_DIM // 2, 1))
                t = y * cos + rot * sin
                if is_q:
                    t = t * (QK_DIM ** -0.5 * LOG2E)
            elif 3 * N_HEADS <= grp < 4 * N_HEADS:
                t = t * (HEAD_W ** -0.5 * LOG2E)
            o_ref[0, grp] = t.astype(bf16)


def _inproj(x2d, g_mix, w_in, cos_t, sin_t, qg, kg, *, batch, seq, tm, tn):
    rows, d = x2d.shape
    nb = seq // tm
    return pl.pallas_call(
        functools.partial(_inproj_kernel, tn=tn),
        grid=(rows // tm,),
        in_specs=[
            pl.BlockSpec((tm, d), lambda i: (i, 0)),
            pl.BlockSpec((1, d), lambda i: (0, 0)),
            pl.BlockSpec(w_in.shape, lambda i: (0, 0), pipeline_mode=pl.Buffered(1)),
            pl.BlockSpec((tm, HEAD_W), lambda i: (i % nb, 0)),
            pl.BlockSpec((tm, HEAD_W), lambda i: (i % nb, 0)),
            pl.BlockSpec((1, HEAD_W), lambda i: (0, 0)),
            pl.BlockSpec((1, HEAD_W), lambda i: (0, 0)),
        ],
        out_specs=pl.BlockSpec((1, N_GROUPS, tm, HEAD_W), lambda i: (i // nb, 0, i % nb, 0)),
        out_shape=jax.ShapeDtypeStruct((batch, N_GROUPS, seq, HEAD_W), bf16),
        compiler_params=pltpu.CompilerParams(dimension_semantics=("parallel",), vmem_limit_bytes=VMEM_LIMIT),
        name="inproj",
    )(x2d, g_mix, w_in, cos_t, sin_t, qg, kg)


def _attn_specs(seq, t, hps, q_grp, k_grp, v_grp):
    return [
        pl.BlockSpec((1, hps, t, HEAD_W), lambda b, h, i: (b, q_grp // hps + h, i, 0)),
        pl.BlockSpec((1, hps, seq, HEAD_W), lambda b, h, i: (b, k_grp // hps + h, 0, 0)),
        pl.BlockSpec((1, hps, seq, HEAD_W), lambda b, h, i: (b, v_grp // hps + h, 0, 0)),
        pl.BlockSpec((1, hps, META_PAD, HEAD_W), lambda b, h, i: (0, k_grp // hps + h, 0, 0)),
        pl.BlockSpec((1, hps, META_PAD, HEAD_W), lambda b, h, i: (0, v_grp // hps + h, 0, 0)),
    ]


def _attn_call(kernel_fn, name, small_specs, small_args, proj, proj_meta, grp_specs, scratch, *, t, hps):
    batch, _, seq, _ = proj.shape
    nq = seq // t
    return pl.pallas_call(
        functools.partial(kernel_fn, t=t, hps=hps),
        grid=(batch, N_HEADS // hps, nq),
        in_specs=small_specs + grp_specs,
        out_specs=pl.BlockSpec((t, hps * HEAD_W), lambda b, h, i: (b * nq + i, h)),
        out_shape=jax.ShapeDtypeStruct((batch * seq, N_HEADS * HEAD_W), bf16),
        scratch_shapes=scratch,
        compiler_params=pltpu.CompilerParams(
            dimension_semantics=("parallel", "parallel", "arbitrary"), vmem_limit_bytes=VMEM_LIMIT),
        name=name,
    )(*small_args, proj, proj, proj, proj_meta, proj_meta)


def _diff_kernel(lq1_ref, lk1_ref, lq2_ref, lk2_ref, og_ref, q_ref, k_ref, v_ref, km_ref, vm_ref, o_ref,
                 s_meta, s_a, s_b, *, t, hps):
    qi = pl.program_id(2)
    heads = range(hps)
    lane_q = _lane_iota((t, HEAD_W))
    nt_dims = (((1,), (1,)), ((), ()))
    tn_dims = (((0,), (0,)), ((), ()))

    def stacked_q(h):
        q = q_ref[0, h]
        zero = jnp.zeros_like(q)
        return jnp.concatenate([jnp.where(lane_q < QK_DIM, q, zero), jnp.where(lane_q >= QK_DIM, q, zero)], axis=0)

    qs = [stacked_q(h) for h in heads]

    lam = (jnp.exp(jnp.sum(lq1_ref[...] * lk1_ref[...], axis=-1, keepdims=True))
           - jnp.exp(jnp.sum(lq2_ref[...] * lk2_ref[...], axis=-1, keepdims=True))
           + LAMBDA_INIT)

    key = lax.broadcasted_iota(jnp.int32, (t, 2 * t), 0)
    qry = lax.broadcasted_iota(jnp.int32, (t, 2 * t), 1)
    causal = key <= jnp.where(qry >= t, qry - t, qry)

    def produce(buf, kblks, mask, m_in):
        m_out = []
        for h in heads:
            s = lax.dot_general(kblks[h], qs[h], nt_dims, preferred_element_type=f32)
            if mask is not None:
                s = jnp.where(mask, s, NEG_INF)
            buf[h] = s
            m_out.append(jnp.maximum(m_in[h], jnp.max(s, axis=0, keepdims=True)))
        return m_out

    def k_block(j):
        start = pl.multiple_of(j * t, t)
        return [k_ref[0, h, pl.ds(start, t), :] for h in heads]

    def consume(buf, vblks, state):
        m_prev, m_cur, l, acc = state
        alphas, ps, ls = [], [], []
        for h in heads:
            alpha = jnp.exp2(m_prev[h] - m_cur[h])
            p = jnp.exp2(buf[h] - m_cur[h])
            ls.append(alpha * l[h] + jnp.sum(p, axis=0, keepdims=True))
            alphas.append(alpha)
            ps.append(p.astype(bf16))
        accs = [alphas[h] * acc[h] + lax.dot_general(vblks[h], ps[h], tn_dims, preferred_element_type=f32)
                for h in heads]
        return ls, accs

    def v_block(j):
        start = pl.multiple_of(j * t, t)
        return [v_ref[0, h, pl.ds(start, t), :] for h in heads]

    def step(cur, nxt, j, state):
        m_next = produce(nxt, k_block(j - 1), None, state[1])
        l, acc = consume(cur, v_block(j), state)
        return (state[1], m_next, l, acc)

    def meta_steps(cur, state):
        meta_mask = lax.broadcasted_iota(jnp.int32, (META_PAD, 2 * t), 0) < N_META
        m_meta = produce(s_meta, [km_ref[0, h] for h in heads], meta_mask, state[1])
        l, acc = consume(cur, v_block(0), state)
        return consume(s_meta, [vm_ref[0, h] for h in heads], (state[1], m_meta, l, acc))

    neg = [jnp.full((1, 2 * t), NEG_INF, f32) for _ in heads]
    m_diag = produce(s_a, k_block(qi), causal, neg)
    state = (neg, m_diag, [jnp.zeros((1, 2 * t), f32) for _ in heads],
             [jnp.zeros((HEAD_W, 2 * t), f32) for _ in heads])

    def pair(n, st):
        st = step(s_a, s_b, qi - 2 * n, st)
        return step(s_b, s_a, qi - 2 * n - 1, st)

    state = lax.fori_loop(0, qi // 2, pair, state)

    def odd_tail(st):
        return meta_steps(s_b, step(s_a, s_b, 1, st))

    def even_tail(st):
        return meta_steps(s_a, st)

    l_fin, acc_fin = lax.cond(qi % 2 == 1, odd_tail, even_tail, state)

    for h in heads:
        o = acc_fin[h] / l_fin[h]
        o = o[:, :t] - lam * o[:, t:]
        ms = jnp.mean(o * o, axis=0, keepdims=True)
        o = (o * lax.rsqrt(ms + EPS)).T
        o_ref[:, h * HEAD_W:(h + 1) * HEAD_W] = (o * og_ref[...] * (1.0 - LAMBDA_INIT)).astype(bf16)


def _diff_attention(proj, proj_meta, lq1, lk1, lq2, lk2, og, *, t):
    vec = lambda n: pl.BlockSpec((1, n), lambda b, h, i: (0, 0))
    return _attn_call(
        _diff_kernel, "diff_attn",
        [vec(QK_DIM), vec(QK_DIM), vec(QK_DIM), vec(QK_DIM), vec(HEAD_W)], (lq1, lk1, lq2, lk2, og),
        proj, proj_meta, _attn_specs(proj.shape[2], t, DIFF_HEADS_PER_STEP, 0, N_HEADS, 2 * N_HEADS),
        [pltpu.VMEM((DIFF_HEADS_PER_STEP, META_PAD, 2 * t), f32),
         pltpu.VMEM((DIFF_HEADS_PER_STEP, t, 2 * t), f32),
         pltpu.VMEM((DIFF_HEADS_PER_STEP, t, 2 * t), f32)],
        t=t, hps=DIFF_HEADS_PER_STEP)


def _sb_kernel(og_ref, q_ref, k_ref, v_ref, km_ref, vm_ref, o_ref, d_meta, d_a, d_b, *, t, hps):
    qi = pl.program_id(2)
    heads = range(hps)
    nt_dims = (((1,), (1,)), ((), ()))
    tn_dims = (((0,), (0,)), ((), ()))
    qh = [q_ref[0, h] for h in heads]

    def later_key_matrix(n):
        r = lax.broadcasted_iota(jnp.int32, (n, 2 * n), 0)
        c = lax.broadcasted_iota(jnp.int32, (n, 2 * n), 1)
        return (jnp.where(c >= n, c - n, c) > r).astype(bf16)

    def produce(buf, kblks, lmat, valid):
        zs = [lax.dot_general(kblks[h], qh[h], nt_dims, preferred_element_type=f32) for h in heads]
        if valid is not None:
            zs = [jnp.where(valid, z, NEG_INF) for z in zs]
        sps, laters = [], []
        for h in heads:
            neg_abs = lax.bitcast_convert_type(
                lax.bitcast_convert_type(zs[h], jnp.uint32) | jnp.uint32(0x80000000), f32)
            sp = jnp.maximum(zs[h], 0.0) + jnp.log(1.0 + jnp.exp2(neg_abs)) * LOG2E
            hi = sp.astype(bf16)
            lo = (sp - hi.astype(f32)).astype(bf16)
            sps.append(sp)
            laters.append(jnp.dot(lmat, jnp.concatenate([hi, lo], axis=0), preferred_element_type=f32))
        totals = []
        for h in heads:
            buf[h] = zs[h] - sps[h] - laters[h]
            totals.append(laters[h][0:1, :] + sps[h][0:1, :])
        return totals

    def consume(buf, vblks, run, acc):
        a = [jnp.exp2(buf[h] - run[h]).astype(bf16) for h in heads]
        return [acc[h] + lax.dot_general(vblks[h], a[h], tn_dims, preferred_element_type=f32) for h in heads]

    def k_block(j):
        start = pl.multiple_of(j * t, t)
        return [k_ref[0, h, pl.ds(start, t), :] for h in heads]

    def v_block(j):
        start = pl.multiple_of(j * t, t)
        return [v_ref[0, h, pl.ds(start, t), :] for h in heads]

    lmat_t = later_key_matrix(t)

    def step(cur, nxt, j, state):
        run_c, run_n, acc = state
        totals = produce(nxt, k_block(j - 1), lmat_t, None)
        acc = consume(cur, v_block(j), run_c, acc)
        return (run_n, [run_n[h] + totals[h] for h in heads], acc)

    def meta_steps(cur, state):
        run_c, run_n, acc = state
        valid = lax.broadcasted_iota(jnp.int32, (META_PAD, t), 0) < N_META
        produce(d_meta, [km_ref[0, h] for h in heads], later_key_matrix(META_PAD), valid)
        acc = consume(cur, v_block(0), run_c, acc)
        return consume(d_meta, [vm_ref[0, h] for h in heads], run_n, acc)

    key = lax.broadcasted_iota(jnp.int32, (t, t), 0)
    qry = lax.broadcasted_iota(jnp.int32, (t, t), 1)
    totals = produce(d_a, k_block(qi), lmat_t, key < qry)
    state = ([jnp.zeros((1, t), f32) for _ in heads], totals, [jnp.zeros((HEAD_W, t), f32) for _ in heads])

    def pair(n, st):
        st = step(d_a, d_b, qi - 2 * n, st)
        return step(d_b, d_a, qi - 2 * n - 1, st)

    state = lax.fori_loop(0, qi // 2, pair, state)

    def odd_tail(st):
        return meta_steps(d_b, step(d_a, d_b, 1, st))

    def even_tail(st):
        return meta_steps(d_a, st)

    acc_fin = lax.cond(qi % 2 == 1, odd_tail, even_tail, state)
    for h in heads:
        acc = acc_fin[h]
        ms = jnp.mean(acc * acc, axis=0, keepdims=True)
        o = (acc * lax.rsqrt(ms + EPS)).T
        o_ref[:, h * HEAD_W:(h + 1) * HEAD_W] = (o * og_ref[...]).astype(bf16)


def _sb_attention(proj, proj_meta, og, *, t):
    return _attn_call(
        _sb_kernel, "sb_attn", [pl.BlockSpec((1, HEAD_W), lambda b, h, i: (0, 0))], (og,),
        proj, proj_meta, _attn_specs(proj.shape[2], t, SB_HEADS_PER_STEP, 3 * N_HEADS, 4 * N_HEADS, 5 * N_HEADS),
        [pltpu.VMEM((SB_HEADS_PER_STEP, META_PAD, t), f32),
         pltpu.VMEM((SB_HEADS_PER_STEP, t, t), f32),
         pltpu.VMEM((SB_HEADS_PER_STEP, t, t), f32)],
        t=t, hps=SB_HEADS_PER_STEP)


def _outproj_kernel(x_ref, md_ref, ms_ref, wd_ref, ws_ref, o_ref):
    o_ref[...] = (x_ref[...]
                  + jnp.dot(md_ref[...], wd_ref[...], preferred_element_type=f32)
                  + jnp.dot(ms_ref[...], ws_ref[...], preferred_element_type=f32))


def _outproj(x2d, mixed_diff, mixed_sb, w_out, *, tm):
    rows, d = x2d.shape
    half = mixed_diff.shape[1]
    return pl.pallas_call(
        _outproj_kernel,
        grid=(rows // tm,),
        in_specs=[
            pl.BlockSpec((tm, d), lambda i: (i, 0)),
            pl.BlockSpec((tm, half), lambda i: (i, 0)),
            pl.BlockSpec((tm, half), lambda i: (i, 0)),
            pl.BlockSpec((half, d), lambda i: (0, 0)),
            pl.BlockSpec((half, d), lambda i: (1, 0)),
        ],
        out_specs=pl.BlockSpec((tm, d), lambda i: (i, 0)),
        out_shape=jax.ShapeDtypeStruct((rows, d), f32),
        compiler_params=pltpu.CompilerParams(
            dimension_semantics=("parallel",), vmem_limit_bytes=VMEM_LIMIT),
        name="outproj",
    )(x2d, mixed_diff, mixed_sb, w_out, w_out)


def _mlp_kernel(h_ref, g_ref, wu_ref, wd_ref, o_ref, m_ref):
    @pl.when(pl.program_id(1) == 0)
    def _():
        h = h_ref[...]
        ms = jnp.mean(h * h, axis=-1, keepdims=True)
        m_ref[...] = (h * lax.rsqrt(ms + EPS) * g_ref[...]).astype(bf16)
        o_ref[...] = h

    hid = jnp.dot(m_ref[...], wu_ref[...], preferred_element_type=f32)
    hid = jnp.square(jnp.maximum(hid, 0.0))
    o_ref[...] += jnp.dot(hid.astype(bf16), wd_ref[...], preferred_element_type=f32)


def _mlp(h1, g_mlp, w_up, w_down, *, tm, tf):
    rows, d = h1.shape
    return pl.pallas_call(
        _mlp_kernel,
        grid=(rows // tm, w_up.shape[1] // tf),
        in_specs=[
            pl.BlockSpec((tm, d), lambda i, f: (i, 0)),
            pl.BlockSpec((1, d), lambda i, f: (0, 0)),
            pl.BlockSpec((d, tf), lambda i, f: (0, f)),
            pl.BlockSpec((tf, d), lambda i, f: (f, 0)),
        ],
        out_specs=pl.BlockSpec((tm, d), lambda i, f: (i, 0)),
        out_shape=jax.ShapeDtypeStruct((rows, d), f32),
        scratch_shapes=[pltpu.VMEM((tm, d), bf16)],
        compiler_params=pltpu.CompilerParams(
            dimension_semantics=("parallel", "arbitrary"), vmem_limit_bytes=VMEM_LIMIT),
        name="mlp",
    )(h1, g_mlp, w_up, w_down)


def _rope_tables(n_pos):
    pos = jnp.arange(n_pos, dtype=f32)
    inv = ROPE_THETA ** (-jnp.arange(0, QK_DIM, 2, dtype=f32) / QK_DIM)
    ang = pos[:, None] * inv[None, :]
    cos, sin = jnp.cos(ang), jnp.sin(ang)
    return jnp.concatenate([cos, cos, cos, cos], axis=-1), jnp.concatenate([-sin, sin, -sin, sin], axis=-1)


def kernel(x, meta_tokens, g_mix, w_in, q_norm_g, k_norm_g, lambda_q1, lambda_k1, lambda_q2, lambda_k2,
           diff_out_g, sb_out_g, w_out, g_mlp, w_up, w_down):
    batch, seq, d = x.shape
    assert g_mix.shape[0] == 1, "single-layer kernel"
    assert meta_tokens.shape[0] == N_META and seq % ATTN_BLOCK == 0

    x2d = x.reshape(batch * seq, d)
    w_in_b = w_in[0].astype(bf16)
    w_out_b = w_out[0].astype(bf16)
    w_up_b = w_up[0].astype(bf16)
    w_down_b = w_down[0].astype(bf16)
    qg = jnp.tile(q_norm_g[0], 2)[None, :]
    kg = jnp.tile(k_norm_g[0], 2)[None, :]
    cos_t, sin_t = _rope_tables(N_META + seq)

    proj = _inproj(x2d, g_mix, w_in_b, cos_t[N_META:], sin_t[N_META:], qg, kg,
                   batch=batch, seq=seq, tm=256, tn=512)
    proj_meta = _inproj(meta_tokens.astype(f32), g_mix, w_in_b, cos_t[:N_META], sin_t[:N_META], qg, kg,
                        batch=1, seq=N_META, tm=N_META, tn=512)
    proj_meta = jnp.pad(proj_meta, ((0, 0), (0, 0), (0, META_PAD - N_META), (0, 0)))

    mixed_diff = _diff_attention(proj, proj_meta, lambda_q1, lambda_k1, lambda_q2, lambda_k2, diff_out_g,
                                 t=ATTN_BLOCK)
    mixed_sb = _sb_attention(proj, proj_meta, sb_out_g, t=ATTN_BLOCK)

    h1 = _outproj(x2d, mixed_diff, mixed_sb, w_out_b, tm=512)
    out = _mlp(h1, g_mlp, w_up_b, w_down_b, tm=512, tf=512)
    return out.reshape(batch, seq, d)
```

```python
import functools
import math

import jax
import jax.numpy as jnp
from jax import lax
from jax.experimental import pallas as pl
from jax.experimental.pallas import tpu as pltpu

N_META = 16
N_HEADS = 8
HEAD_W = 128
QK_DIM = 64
N_GROUPS = 6 * N_HEADS
ROPE_THETA = 10000.0
EPS = 1e-6
NEG_INF = -1e30
LAMBDA_INIT = 0.8 - 0.6 * math.exp(0.0)
LOG2E = math.log2(math.e)

META_PAD = 128
DIFF_BLOCK = 512
SB_BLOCK = 256
DIFF_HEADS_PER_STEP = 2
SB_HEADS_PER_STEP = 4
VMEM_LIMIT = 56 * 1024 * 1024

f32 = jnp.float32
bf16 = jnp.bfloat16


def _lane_iota(shape):
    return lax.broadcasted_iota(jnp.int32, shape, len(shape) - 1)


def _inproj_kernel(x_ref, g_ref, w_ref, cos_ref, sin_ref, qg_ref, kg_ref, o_ref, *, tn):
    xf = x_ref[...]
    ms = jnp.mean(xf * xf, axis=-1, keepdims=True)
    u = (xf * lax.rsqrt(ms + EPS) * g_ref[...]).astype(bf16)

    ngrp = tn // HEAD_W
    lane = _lane_iota((1, HEAD_W))
    first = lane < QK_DIM
    low_half = (lane & (QK_DIM // 2)) == 0
    cos = cos_ref[...]
    sin = sin_ref[...]

    for j in range(w_ref.shape[1] // tn):
        acc = jnp.dot(u, w_ref[:, j * tn:(j + 1) * tn], preferred_element_type=f32)
        for g in range(ngrp):
            grp = j * ngrp + g
            t = acc[:, g * HEAD_W:(g + 1) * HEAD_W]
            if grp < 2 * N_HEADS:
                is_q = grp < N_HEADS
                ss = t * t
                tot = jnp.sum(ss, axis=-1, keepdims=True)
                lo = jnp.sum(jnp.where(first, ss, 0.0), axis=-1, keepdims=True)
                msq = jnp.where(first, lo, tot - lo) * (1.0 / QK_DIM)
                y = t * lax.rsqrt(msq + EPS) * (qg_ref[...] if is_q else kg_ref[...])
                rot = jnp.where(low_half, pltpu.roll(y, HEAD_W - QK_DIM // 2, 1), pltpu.roll(y, QK_DIM // 2, 1))
                t = y * cos + rot * sin
                if is_q:
                    t = t * (QK_DIM ** -0.5 * LOG2E)
            elif 3 * N_HEADS <= grp < 4 * N_HEADS:
                t = t * (HEAD_W ** -0.5 * LOG2E)
            o_ref[0, grp] = t.astype(bf16)


def _inproj(x2d, g_mix, w_in, cos_t, sin_t, qg, kg, *, batch, seq, tm, tn):
    rows, d = x2d.shape
    nb = seq // tm
    return pl.pallas_call(
        functools.partial(_inproj_kernel, tn=tn),
        grid=(rows // tm,),
        in_specs=[
            pl.BlockSpec((tm, d), lambda i: (i, 0)),
            pl.BlockSpec((1, d), lambda i: (0, 0)),
            pl.BlockSpec(w_in.shape, lambda i: (0, 0), pipeline_mode=pl.Buffered(1)),
            pl.BlockSpec((tm, HEAD_W), lambda i: (i % nb, 0)),
            pl.BlockSpec((tm, HEAD_W), lambda i: (i % nb, 0)),
            pl.BlockSpec((1, HEAD_W), lambda i: (0, 0)),
            pl.BlockSpec((1, HEAD_W), lambda i: (0, 0)),
        ],
        out_specs=pl.BlockSpec((1, N_GROUPS, tm, HEAD_W), lambda i: (i // nb, 0, i % nb, 0)),
        out_shape=jax.ShapeDtypeStruct((batch, N_GROUPS, seq, HEAD_W), bf16),
        compiler_params=pltpu.CompilerParams(dimension_semantics=("parallel",), vmem_limit_bytes=VMEM_LIMIT),
        name="inproj",
    )(x2d, g_mix, w_in, cos_t, sin_t, qg, kg)


def _attn_specs(seq, t, hps, q_grp, k_grp, v_grp):
    return [
        pl.BlockSpec((1, hps, t, HEAD_W), lambda b, h, i: (b, q_grp // hps + h, i, 0)),
        pl.BlockSpec((1, hps, seq, HEAD_W), lambda b, h, i: (b, k_grp // hps + h, 0, 0)),
        pl.BlockSpec((1, hps, seq, HEAD_W), lambda b, h, i: (b, v_grp // hps + h, 0, 0)),
        pl.BlockSpec((1, hps, META_PAD, HEAD_W), lambda b, h, i: (0, k_grp // hps + h, 0, 0)),
        pl.BlockSpec((1, hps, META_PAD, HEAD_W), lambda b, h, i: (0, v_grp // hps + h, 0, 0)),
    ]


def _attn_call(kernel_fn, name, small_specs, small_args, proj, proj_meta, grp_specs, scratch, *, t, hps):
    batch, _, seq, _ = proj.shape
    nq = seq // t
    return pl.pallas_call(
        functools.partial(kernel_fn, t=t, hps=hps),
        grid=(batch, N_HEADS // hps, nq),
        in_specs=small_specs + grp_specs,
        out_specs=pl.BlockSpec((t, hps * HEAD_W), lambda b, h, i: (b * nq + i, h)),
        out_shape=jax.ShapeDtypeStruct((batch * seq, N_HEADS * HEAD_W), bf16),
        scratch_shapes=scratch,
        compiler_params=pltpu.CompilerParams(
            dimension_semantics=("parallel", "parallel", "arbitrary"), vmem_limit_bytes=VMEM_LIMIT),
        name=name,
    )(*small_args, proj, proj, proj, proj_meta, proj_meta)


def _diff_kernel(lq1_ref, lk1_ref, lq2_ref, lk2_ref, og_ref, q_ref, k_ref, v_ref, km_ref, vm_ref, o_ref,
                 s_meta, s_a, s_b, *, t, hps):
    qi = pl.program_id(2)
    heads = range(hps)
    lane_q = _lane_iota((t, HEAD_W))
    nt_dims = (((1,), (1,)), ((), ()))
    tn_dims = (((0,), (0,)), ((), ()))

    def stacked_q(h):
        q = q_ref[0, h]
        zero = jnp.zeros_like(q)
        return jnp.concatenate([jnp.where(lane_q < QK_DIM, q, zero), jnp.where(lane_q >= QK_DIM, q, zero)], axis=0)

    qs = [stacked_q(h) for h in heads]

    lam = (jnp.exp(jnp.sum(lq1_ref[...] * lk1_ref[...], axis=-1, keepdims=True))
           - jnp.exp(jnp.sum(lq2_ref[...] * lk2_ref[...], axis=-1, keepdims=True))
           + LAMBDA_INIT)

    key = lax.broadcasted_iota(jnp.int32, (t, 2 * t), 0)
    qry = lax.broadcasted_iota(jnp.int32, (t, 2 * t), 1)
    causal = key <= jnp.where(qry >= t, qry - t, qry)

    def produce(buf, kblks, mask, m_in):
        m_out = []
        for h in heads:
            s = lax.dot_general(kblks[h], qs[h], nt_dims, preferred_element_type=f32)
            if mask is not None:
                s = jnp.where(mask, s, NEG_INF)
            buf[h] = s
            m_out.append(jnp.maximum(m_in[h], jnp.max(s, axis=0, keepdims=True)))
        return m_out

    def k_block(j):
        start = pl.multiple_of(j * t, t)
        return [k_ref[0, h, pl.ds(start, t), :] for h in heads]

    def consume(buf, vblks, state):
        m_prev, m_cur, l, acc = state
        alphas, ps, ls = [], [], []
        for h in heads:
            alpha = jnp.exp2(m_prev[h] - m_cur[h])
            p = jnp.exp2(buf[h] - m_cur[h])
            ls.append(alpha * l[h] + jnp.sum(p, axis=0, keepdims=True))
            alphas.append(alpha)
            ps.append(p.astype(bf16))
        accs = [alphas[h] * acc[h] + lax.dot_general(vblks[h], ps[h], tn_dims, preferred_element_type=f32)
                for h in heads]
        return ls, accs

    def v_block(j):
        start = pl.multiple_of(j * t, t)
        return [v_ref[0, h, pl.ds(start, t), :] for h in heads]

    def step(cur, nxt, j, state):
        m_next = produce(nxt, k_block(j - 1), None, state[1])
        l, acc = consume(cur, v_block(j), state)
        return (state[1], m_next, l, acc)

    def meta_steps(cur, state):
        meta_mask = lax.broadcasted_iota(jnp.int32, (META_PAD, 2 * t), 0) < N_META
        m_meta = produce(s_meta, [km_ref[0, h] for h in heads], meta_mask, state[1])
        l, acc = consume(cur, v_block(0), state)
        return consume(s_meta, [vm_ref[0, h] for h in heads], (state[1], m_meta, l, acc))

    neg = [jnp.full((1, 2 * t), NEG_INF, f32) for _ in heads]
    m_diag = produce(s_a, k_block(qi), causal, neg)
    state = (neg, m_diag, [jnp.zeros((1, 2 * t), f32) for _ in heads],
             [jnp.zeros((HEAD_W, 2 * t), f32) for _ in heads])

    def pair(n, st):
        st = step(s_a, s_b, qi - 2 * n, st)
        return step(s_b, s_a, qi - 2 * n - 1, st)

    state = lax.fori_loop(0, qi // 2, pair, state)

    def odd_tail(st):
        return meta_steps(s_b, step(s_a, s_b, 1, st))

    def even_tail(st):
        return meta_steps(s_a, st)

    l_fin, acc_fin = lax.cond(qi % 2 == 1, odd_tail, even_tail, state)

    for h in heads:
        o = acc_fin[h] / l_fin[h]
        o = o[:, :t] - lam * o[:, t:]
        ms = jnp.mean(o * o, axis=0, keepdims=True)
        o = (o * lax.rsqrt(ms + EPS)).T
        o_ref[:, h * HEAD_W:(h + 1) * HEAD_W] = (o * og_ref[...] * (1.0 - LAMBDA_INIT)).astype(bf16)


def _diff_attention(proj, proj_meta, lq1, lk1, lq2, lk2, og, *, t):
    vec = lambda n: pl.BlockSpec((1, n), lambda b, h, i: (0, 0))
    return _attn_call(
        _diff_kernel, "diff_attn",
        [vec(QK_DIM), vec(QK_DIM), vec(QK_DIM), vec(QK_DIM), vec(HEAD_W)], (lq1, lk1, lq2, lk2, og),
        proj, proj_meta, _attn_specs(proj.shape[2], t, DIFF_HEADS_PER_STEP, 0, N_HEADS, 2 * N_HEADS),
        [pltpu.VMEM((DIFF_HEADS_PER_STEP, META_PAD, 2 * t), f32),
         pltpu.VMEM((DIFF_HEADS_PER_STEP, t, 2 * t), f32),
         pltpu.VMEM((DIFF_HEADS_PER_STEP, t, 2 * t), f32)],
        t=t, hps=DIFF_HEADS_PER_STEP)


def _sb_kernel(og_ref, q_ref, k_ref, v_ref, km_ref, vm_ref, o_ref, d_meta, d_a, d_b, *, t, hps):
    qi = pl.program_id(2)
    heads = range(hps)
    nt_dims = (((1,), (1,)), ((), ()))
    tn_dims = (((0,), (0,)), ((), ()))
    qh = [q_ref[0, h] for h in heads]

    def later_key_matrix(n):
        r = lax.broadcasted_iota(jnp.int32, (n, n), 0)
        c = lax.broadcasted_iota(jnp.int32, (n, n), 1)
        return (c > r).astype(bf16)

    def scores(kblks, valid):
        zs = [lax.dot_general(kblks[h], qh[h], nt_dims, preferred_element_type=f32) for h in heads]
        if valid is not None:
            zs = [jnp.where(valid, z, NEG_INF) for z in zs]
        return zs

    def produce(buf, zs, lmat):
        sps, laters = [], []
        for h in heads:
            sp = jnp.maximum(zs[h], 0.0) + jnp.log(1.0 + jnp.exp2(-jnp.abs(zs[h]))) * LOG2E
            sps.append(sp)
            laters.append(jnp.dot(lmat, sp.astype(bf16), preferred_element_type=f32))
        totals = []
        for h in heads:
            buf[h] = zs[h] - sps[h] - laters[h]
            totals.append(laters[h][0:1, :] + sps[h][0:1, :])
        return totals

    def consume(buf, vblks, run, acc):
        a = [jnp.exp2(buf[h] - run[h]).astype(bf16) for h in heads]
        return [acc[h] + lax.dot_general(vblks[h], a[h], tn_dims, preferred_element_type=f32) for h in heads]

    def k_block(j):
        start = pl.multiple_of(j * t, t)
        return [k_ref[0, h, pl.ds(start, t), :] for h in heads]

    def v_block(j):
        start = pl.multiple_of(j * t, t)
        return [v_ref[0, h, pl.ds(start, t), :] for h in heads]

    lmat_t = later_key_matrix(t)

    def step(cur, nxt, j, state):
        run_c, run_n, acc = state
        zs = scores(k_block(j - 1), None)
        acc = consume(cur, v_block(j), run_c, acc)
        totals = produce(nxt, zs, lmat_t)
        return (run_n, [run_n[h] + totals[h] for h in heads], acc)

    def meta_steps(cur, state):
        run_c, run_n, acc = state
        zs = scores([km_ref[0, h] for h in heads], lax.broadcasted_iota(jnp.int32, (META_PAD, t), 0) < N_META)
        acc = consume(cur, v_block(0), run_c, acc)
        produce(d_meta, zs, later_key_matrix(META_PAD))
        return consume(d_meta, [vm_ref[0, h] for h in heads], run_n, acc)

    key = lax.broadcasted_iota(jnp.int32, (t, t), 0)
    qry = lax.broadcasted_iota(jnp.int32, (t, t), 1)
    totals = produce(d_a, scores(k_block(qi), key < qry), lmat_t)
    state = ([jnp.zeros((1, t), f32) for _ in heads], totals, [jnp.zeros((HEAD_W, t), f32) for _ in heads])

    def pair(n, st):
        st = step(d_a, d_b, qi - 2 * n, st)
        return step(d_b, d_a, qi - 2 * n - 1, st)

    state = lax.fori_loop(0, qi // 2, pair, state)

    def odd_tail(st):
        return meta_steps(d_b, step(d_a, d_b, 1, st))

    def even_tail(st):
        return meta_steps(d_a, st)

    acc_fin = lax.cond(qi % 2 == 1, odd_tail, even_tail, state)
    for h in heads:
        acc = acc_fin[h]
        ms = jnp.mean(acc * acc, axis=0, keepdims=True)
        o = (acc * lax.rsqrt(ms + EPS)).T
        o_ref[:, h * HEAD_W:(h + 1) * HEAD_W] = (o * og_ref[...]).astype(bf16)


def _sb_attention(proj, proj_meta, og, *, t):
    return _attn_call(
        _sb_kernel, "sb_attn", [pl.BlockSpec((1, HEAD_W), lambda b, h, i: (0, 0))], (og,),
        proj, proj_meta, _attn_specs(proj.shape[2], t, SB_HEADS_PER_STEP, 3 * N_HEADS, 4 * N_HEADS, 5 * N_HEADS),
        [pltpu.VMEM((SB_HEADS_PER_STEP, META_PAD, t), f32),
         pltpu.VMEM((SB_HEADS_PER_STEP, t, t), f32),
         pltpu.VMEM((SB_HEADS_PER_STEP, t, t), f32)],
        t=t, hps=SB_HEADS_PER_STEP)


def _outproj_kernel(x_ref, md_ref, ms_ref, wd_ref, ws_ref, o_ref):
    o_ref[...] = (x_ref[...]
                  + jnp.dot(md_ref[...], wd_ref[...], preferred_element_type=f32)
                  + jnp.dot(ms_ref[...], ws_ref[...], preferred_element_type=f32))


def _outproj(x2d, mixed_diff, mixed_sb, w_out, *, tm):
    rows, d = x2d.shape
    half = mixed_diff.shape[1]
    return pl.pallas_call(
        _outproj_kernel,
        grid=(rows // tm,),
        in_specs=[
            pl.BlockSpec((tm, d), lambda i: (i, 0)),
            pl.BlockSpec((tm, half), lambda i: (i, 0)),
            pl.BlockSpec((tm, half), lambda i: (i, 0)),
            pl.BlockSpec((half, d), lambda i: (0, 0)),
            pl.BlockSpec((half, d), lambda i: (1, 0)),
        ],
        out_specs=pl.BlockSpec((tm, d), lambda i: (i, 0)),
        out_shape=jax.ShapeDtypeStruct((rows, d), f32),
        compiler_params=pltpu.CompilerParams(
            dimension_semantics=("parallel",), vmem_limit_bytes=VMEM_LIMIT),
        name="outproj",
    )(x2d, mixed_diff, mixed_sb, w_out, w_out)


def _mlp_kernel(h_ref, g_ref, wu_ref, wd_ref, o_ref, m_ref):
    @pl.when(pl.program_id(1) == 0)
    def _():
        h = h_ref[...]
        ms = jnp.mean(h * h, axis=-1, keepdims=True)
        m_ref[...] = (h * lax.rsqrt(ms + EPS) * g_ref[...]).astype(bf16)
        o_ref[...] = h

    hid = jnp.dot(m_ref[...], wu_ref[...], preferred_element_type=f32)
    hid = jnp.square(jnp.maximum(hid, 0.0))
    o_ref[...] += jnp.dot(hid.astype(bf16), wd_ref[...], preferred_element_type=f32)


def _mlp(h1, g_mlp, w_up, w_down, *, tm, tf):
    rows, d = h1.shape
    return pl.pallas_call(
        _mlp_kernel,
        grid=(rows // tm, w_up.shape[1] // tf),
        in_specs=[
            pl.BlockSpec((tm, d), lambda i, f: (i, 0)),
            pl.BlockSpec((1, d), lambda i, f: (0, 0)),
            pl.BlockSpec((d, tf), lambda i, f: (0, f)),
            pl.BlockSpec((tf, d), lambda i, f: (f, 0)),
        ],
        out_specs=pl.BlockSpec((tm, d), lambda i, f: (i, 0)),
        out_shape=jax.ShapeDtypeStruct((rows, d), f32),
        scratch_shapes=[pltpu.VMEM((tm, d), bf16)],
        compiler_params=pltpu.CompilerParams(
            dimension_semantics=("parallel", "arbitrary"), vmem_limit_bytes=VMEM_LIMIT),
        name="mlp",
    )(h1, g_mlp, w_up, w_down)


def _rope_tables(n_pos):
    pos = jnp.arange(n_pos, dtype=f32)
    inv = ROPE_THETA ** (-jnp.arange(0, QK_DIM, 2, dtype=f32) / QK_DIM)
    ang = pos[:, None] * inv[None, :]
    cos, sin = jnp.cos(ang), jnp.sin(ang)
    return jnp.concatenate([cos, cos, cos, cos], axis=-1), jnp.concatenate([-sin, sin, -sin, sin], axis=-1)


def kernel(x, meta_tokens, g_mix, w_in, q_norm_g, k_norm_g, lambda_q1, lambda_k1, lambda_q2, lambda_k2,
           diff_out_g, sb_out_g, w_out, g_mlp, w_up, w_down):
    batch, seq, d = x.shape
    assert g_mix.shape[0] == 1, "single-layer kernel"
    assert meta_tokens.shape[0] == N_META and seq % DIFF_BLOCK == 0 and seq % SB_BLOCK == 0

    x2d = x.reshape(batch * seq, d)
    w_in_b = w_in[0].astype(bf16)
    w_out_b = w_out[0].astype(bf16)
    w_up_b = w_up[0].astype(bf16)
    w_down_b = w_down[0].astype(bf16)
    qg = jnp.tile(q_norm_g[0], 2)[None, :]
    kg = jnp.tile(k_norm_g[0], 2)[None, :]
    cos_t, sin_t = _rope_tables(N_META + seq)

    proj = _inproj(x2d, g_mix, w_in_b, cos_t[N_META:], sin_t[N_META:], qg, kg,
                   batch=batch, seq=seq, tm=256, tn=512)
    proj_meta = _inproj(meta_tokens.astype(f32), g_mix, w_in_b, cos_t[:N_META], sin_t[:N_META], qg, kg,
                        batch=1, seq=N_META, tm=N_META, tn=512)
    proj_meta = jnp.pad(proj_meta, ((0, 0), (0, 0), (0, META_PAD - N_META), (0, 0)))

    mixed_diff = _diff_attention(proj, proj_meta, lambda_q1, lambda_k1, lambda_q2, lambda_k2, diff_out_g,
                                 t=DIFF_BLOCK)
    mixed_sb = _sb_attention(proj, proj_meta, sb_out_g, t=SB_BLOCK)

    h1 = _outproj(x2d, mixed_diff, mixed_sb, w_out_b, tm=512)
    out = _mlp(h1, g_mlp, w_up_b, w_down_b, tm=512, tf=1024)
    return out.reshape(batch, seq, d)
```

```python
import functools
import math

import jax
import jax.numpy as jnp
from jax import lax
from jax.experimental import pallas as pl
from jax.experimental.pallas import tpu as pltpu

N_META = 16
N_HEADS = 8
HEAD_W = 128
QK_DIM = 64
N_GROUPS = 6 * N_HEADS
ROPE_THETA = 10000.0
EPS = 1e-6
NEG_INF = -1e30
LAMBDA_INIT = 0.8 - 0.6 * math.exp(0.0)
LOG2E = math.log2(math.e)

META_PAD = 128
DIFF_BLOCK = 512
SB_BLOCK = 512
SB_CUMSUM_BLOCK = 256
DIFF_HEADS_PER_STEP = 2
SB_HEADS_PER_STEP = 4
VMEM_LIMIT = 56 * 1024 * 1024

f32 = jnp.float32
bf16 = jnp.bfloat16


def _lane_iota(shape):
    return lax.broadcasted_iota(jnp.int32, shape, len(shape) - 1)


def _inproj_kernel(x_ref, g_ref, w_ref, cos_ref, sin_ref, qg_ref, kg_ref, o_ref, *, tn):
    xf = x_ref[...]
    ms = jnp.mean(xf * xf, axis=-1, keepdims=True)
    u = (xf * lax.rsqrt(ms + EPS) * g_ref[...]).astype(bf16)

    ngrp = tn // HEAD_W
    lane = _lane_iota((1, HEAD_W))
    first = lane < QK_DIM
    low_half = (lane & (QK_DIM // 2)) == 0
    cos = cos_ref[...]
    sin = sin_ref[...]

    for j in range(w_ref.shape[1] // tn):
        acc = jnp.dot(u, w_ref[:, j * tn:(j + 1) * tn], preferred_element_type=f32)
        for g in range(ngrp):
            grp = j * ngrp + g
            t = acc[:, g * HEAD_W:(g + 1) * HEAD_W]
            if grp < 2 * N_HEADS:
                is_q = grp < N_HEADS
                ss = t * t
                tot = jnp.sum(ss, axis=-1, keepdims=True)
                lo = jnp.sum(jnp.where(first, ss, 0.0), axis=-1, keepdims=True)
                msq = jnp.where(first, lo, tot - lo) * (1.0 / QK_DIM)
                y = t * lax.rsqrt(msq + EPS) * (qg_ref[...] if is_q else kg_ref[...])
                rot = jnp.where(low_half, pltpu.roll(y, HEAD_W - QK_DIM // 2, 1), pltpu.roll(y, QK_DIM // 2, 1))
                t = y * cos + rot * sin
                if is_q:
                    t = t * (QK_DIM ** -0.5 * LOG2E)
            elif 3 * N_HEADS <= grp < 4 * N_HEADS:
                t = t * (HEAD_W ** -0.5 * LOG2E)
            o_ref[0, grp] = t.astype(bf16)


def _inproj(x2d, g_mix, w_in, cos_t, sin_t, qg, kg, *, batch, seq, tm, tn):
    rows, d = x2d.shape
    nb = seq // tm
    return pl.pallas_call(
        functools.partial(_inproj_kernel, tn=tn),
        grid=(rows // tm,),
        in_specs=[
            pl.BlockSpec((tm, d), lambda i: (i, 0)),
            pl.BlockSpec((1, d), lambda i: (0, 0)),
            pl.BlockSpec(w_in.shape, lambda i: (0, 0), pipeline_mode=pl.Buffered(1)),
            pl.BlockSpec((tm, HEAD_W), lambda i: (i % nb, 0)),
            pl.BlockSpec((tm, HEAD_W), lambda i: (i % nb, 0)),
            pl.BlockSpec((1, HEAD_W), lambda i: (0, 0)),
            pl.BlockSpec((1, HEAD_W), lambda i: (0, 0)),
        ],
        out_specs=pl.BlockSpec((1, N_GROUPS, tm, HEAD_W), lambda i: (i // nb, 0, i % nb, 0)),
        out_shape=jax.ShapeDtypeStruct((batch, N_GROUPS, seq, HEAD_W), bf16),
        compiler_params=pltpu.CompilerParams(dimension_semantics=("parallel",), vmem_limit_bytes=VMEM_LIMIT),
        name="inproj",
    )(x2d, g_mix, w_in, cos_t, sin_t, qg, kg)


def _attn_specs(seq, t, hps, q_grp, k_grp, v_grp):
    return [
        pl.BlockSpec((1, hps, t, HEAD_W), lambda b, h, i: (b, q_grp // hps + h, i, 0)),
        pl.BlockSpec((1, hps, seq, HEAD_W), lambda b, h, i: (b, k_grp // hps + h, 0, 0)),
        pl.BlockSpec((1, hps, seq, HEAD_W), lambda b, h, i: (b, v_grp // hps + h, 0, 0)),
        pl.BlockSpec((1, hps, META_PAD, HEAD_W), lambda b, h, i: (0, k_grp // hps + h, 0, 0)),
        pl.BlockSpec((1, hps, META_PAD, HEAD_W), lambda b, h, i: (0, v_grp // hps + h, 0, 0)),
    ]


def _attn_call(kernel_fn, name, small_specs, small_args, proj, proj_meta, grp_specs, scratch, *, t, hps):
    batch, _, seq, _ = proj.shape
    nq = seq // t
    return pl.pallas_call(
        functools.partial(kernel_fn, t=t, hps=hps),
        grid=(batch, N_HEADS // hps, nq),
        in_specs=small_specs + grp_specs,
        out_specs=pl.BlockSpec((t, hps * HEAD_W), lambda b, h, i: (b * nq + i, h)),
        out_shape=jax.ShapeDtypeStruct((batch * seq, N_HEADS * HEAD_W), bf16),
        scratch_shapes=scratch,
        compiler_params=pltpu.CompilerParams(
            dimension_semantics=("parallel", "parallel", "arbitrary"), vmem_limit_bytes=VMEM_LIMIT),
        name=name,
    )(*small_args, proj, proj, proj, proj_meta, proj_meta)


def _diff_kernel(lq1_ref, lk1_ref, lq2_ref, lk2_ref, og_ref, q_ref, k_ref, v_ref, km_ref, vm_ref, o_ref,
                 s_meta, s_a, s_b, *, t, hps):
    qi = pl.program_id(2)
    heads = range(hps)
    lane_q = _lane_iota((t, HEAD_W))
    nt_dims = (((1,), (1,)), ((), ()))
    tn_dims = (((0,), (0,)), ((), ()))

    def stacked_q(h):
        q = q_ref[0, h]
        zero = jnp.zeros_like(q)
        return jnp.concatenate([jnp.where(lane_q < QK_DIM, q, zero), jnp.where(lane_q >= QK_DIM, q, zero)], axis=0)

    qs = [stacked_q(h) for h in heads]

    lam = (jnp.exp(jnp.sum(lq1_ref[...] * lk1_ref[...], axis=-1, keepdims=True))
           - jnp.exp(jnp.sum(lq2_ref[...] * lk2_ref[...], axis=-1, keepdims=True))
           + LAMBDA_INIT)

    key = lax.broadcasted_iota(jnp.int32, (t, 2 * t), 0)
    qry = lax.broadcasted_iota(jnp.int32, (t, 2 * t), 1)
    causal = key <= jnp.where(qry >= t, qry - t, qry)

    def produce(buf, kblks, mask, m_in):
        m_out = []
        for h in heads:
            s = lax.dot_general(kblks[h], qs[h], nt_dims, preferred_element_type=f32)
            if mask is not None:
                s = jnp.where(mask, s, NEG_INF)
            buf[h] = s
            m_out.append(jnp.maximum(m_in[h], jnp.max(s, axis=0, keepdims=True)))
        return m_out

    def k_block(j):
        start = pl.multiple_of(j * t, t)
        return [k_ref[0, h, pl.ds(start, t), :] for h in heads]

    def consume(buf, vblks, state):
        m_prev, m_cur, l, acc = state
        alphas, ps, ls = [], [], []
        for h in heads:
            alpha = jnp.exp2(m_prev[h] - m_cur[h])
            p = jnp.exp2(buf[h] - m_cur[h])
            ls.append(alpha * l[h] + jnp.sum(p, axis=0, keepdims=True))
            alphas.append(alpha)
            ps.append(p.astype(bf16))
        accs = [alphas[h] * acc[h] + lax.dot_general(vblks[h], ps[h], tn_dims, preferred_element_type=f32)
                for h in heads]
        return ls, accs

    def v_block(j):
        start = pl.multiple_of(j * t, t)
        return [v_ref[0, h, pl.ds(start, t), :] for h in heads]

    def step(cur, nxt, j, state):
        m_next = produce(nxt, k_block(j - 1), None, state[1])
        l, acc = consume(cur, v_block(j), state)
        return (state[1], m_next, l, acc)

    def last_steps(cur, state):
        l, acc = consume(cur, v_block(0), state)
        return consume(s_meta, [vm_ref[0, h] for h in heads], (state[1], state[1], l, acc))

    neg = [jnp.full((1, 2 * t), NEG_INF, f32) for _ in heads]
    meta_mask = lax.broadcasted_iota(jnp.int32, (META_PAD, 2 * t), 0) < N_META
    m_meta = produce(s_meta, [km_ref[0, h] for h in heads], meta_mask, neg)
    m_diag = produce(s_a, k_block(qi), causal, m_meta)
    state = (neg, m_diag, [jnp.zeros((1, 2 * t), f32) for _ in heads],
             [jnp.zeros((HEAD_W, 2 * t), f32) for _ in heads])

    def pair(n, st):
        st = step(s_a, s_b, qi - 2 * n, st)
        return step(s_b, s_a, qi - 2 * n - 1, st)

    state = lax.fori_loop(0, qi // 2, pair, state)

    def odd_tail(st):
        return last_steps(s_b, step(s_a, s_b, 1, st))

    def even_tail(st):
        return last_steps(s_a, st)

    l_fin, acc_fin = lax.cond(qi % 2 == 1, odd_tail, even_tail, state)

    for h in heads:
        o = acc_fin[h] / l_fin[h]
        o = o[:, :t] - lam * o[:, t:]
        ms = jnp.mean(o * o, axis=0, keepdims=True)
        o = (o * lax.rsqrt(ms + EPS)).T
        o_ref[:, h * HEAD_W:(h + 1) * HEAD_W] = (o * og_ref[...] * (1.0 - LAMBDA_INIT)).astype(bf16)


def _diff_attention(proj, proj_meta, lq1, lk1, lq2, lk2, og, *, t):
    vec = lambda n: pl.BlockSpec((1, n), lambda b, h, i: (0, 0))
    return _attn_call(
        _diff_kernel, "diff_attn",
        [vec(QK_DIM), vec(QK_DIM), vec(QK_DIM), vec(QK_DIM), vec(HEAD_W)], (lq1, lk1, lq2, lk2, og),
        proj, proj_meta, _attn_specs(proj.shape[2], t, DIFF_HEADS_PER_STEP, 0, N_HEADS, 2 * N_HEADS),
        [pltpu.VMEM((DIFF_HEADS_PER_STEP, META_PAD, 2 * t), f32),
         pltpu.VMEM((DIFF_HEADS_PER_STEP, t, 2 * t), f32),
         pltpu.VMEM((DIFF_HEADS_PER_STEP, t, 2 * t), f32)],
        t=t, hps=DIFF_HEADS_PER_STEP)


def _sb_kernel(og_ref, q_ref, k_ref, v_ref, km_ref, vm_ref, o_ref, d_meta, d_a, d_b, *, t, hps):
    qi = pl.program_id(2)
    heads = range(hps)
    nt_dims = (((1,), (1,)), ((), ()))
    tn_dims = (((0,), (0,)), ((), ()))
    qh = [q_ref[0, h] for h in heads]

    def later_key_matrix(n):
        r = lax.broadcasted_iota(jnp.int32, (n, n), 0)
        c = lax.broadcasted_iota(jnp.int32, (n, n), 1)
        return (c > r).astype(bf16)

    def scores(kblks, valid):
        zs = [lax.dot_general(kblks[h], qh[h], nt_dims, preferred_element_type=f32) for h in heads]
        if valid is not None:
            zs = [jnp.where(valid, z, NEG_INF) for z in zs]
        return zs

    def produce(buf, zs, lmat):
        nsub = lmat.shape[0]
        subs = range(zs[0].shape[0] // nsub)
        sps, laters = [], []
        for h in heads:
            sp = jnp.maximum(zs[h], 0.0) + jnp.log(1.0 + jnp.exp2(-jnp.abs(zs[h]))) * LOG2E
            sps.append(sp)
            laters.append([jnp.dot(lmat, sp[i * nsub:(i + 1) * nsub].astype(bf16), preferred_element_type=f32)
                           for i in subs])
        totals = []
        for h in heads:
            after = None
            for i in reversed(subs):
                rows = slice(i * nsub, (i + 1) * nsub)
                later = laters[h][i] if after is None else laters[h][i] + after
                buf[h, rows] = zs[h][rows] - sps[h][rows] - later
                sub_total = laters[h][i][0:1, :] + sps[h][i * nsub:i * nsub + 1, :]
                after = sub_total if after is None else after + sub_total
            totals.append(after)
        return totals

    def consume(buf, vblks, run, acc):
        a = [jnp.exp2(buf[h] - run[h]).astype(bf16) for h in heads]
        return [acc[h] + lax.dot_general(vblks[h], a[h], tn_dims, preferred_element_type=f32) for h in heads]

    def k_block(j):
        start = pl.multiple_of(j * t, t)
        return [k_ref[0, h, pl.ds(start, t), :] for h in heads]

    def v_block(j):
        start = pl.multiple_of(j * t, t)
        return [v_ref[0, h, pl.ds(start, t), :] for h in heads]

    lmat_t = later_key_matrix(min(t, SB_CUMSUM_BLOCK))

    def step(cur, nxt, j, state):
        run_c, run_n, acc = state
        zs = scores(k_block(j - 1), None)
        acc = consume(cur, v_block(j), run_c, acc)
        totals = produce(nxt, zs, lmat_t)
        return (run_n, [run_n[h] + totals[h] for h in heads], acc)

    def last_steps(cur, state):
        run_c, run_n, acc = state
        acc = consume(cur, v_block(0), run_c, acc)
        return consume(d_meta, [vm_ref[0, h] for h in heads], run_n, acc)

    key = lax.broadcasted_iota(jnp.int32, (t, t), 0)
    qry = lax.broadcasted_iota(jnp.int32, (t, t), 1)
    zs_meta = scores([km_ref[0, h] for h in heads], lax.broadcasted_iota(jnp.int32, (META_PAD, t), 0) < N_META)
    zs_diag = scores(k_block(qi), key < qry)
    produce(d_meta, zs_meta, later_key_matrix(META_PAD))
    totals = produce(d_a, zs_diag, lmat_t)
    state = ([jnp.zeros((1, t), f32) for _ in heads], totals, [jnp.zeros((HEAD_W, t), f32) for _ in heads])

    def pair(n, st):
        st = step(d_a, d_b, qi - 2 * n, st)
        return step(d_b, d_a, qi - 2 * n - 1, st)

    state = lax.fori_loop(0, qi // 2, pair, state)

    def odd_tail(st):
        return last_steps(d_b, step(d_a, d_b, 1, st))

    def even_tail(st):
        return last_steps(d_a, st)

    acc_fin = lax.cond(qi % 2 == 1, odd_tail, even_tail, state)
    for h in heads:
        acc = acc_fin[h]
        ms = jnp.mean(acc * acc, axis=0, keepdims=True)
        o = (acc * lax.rsqrt(ms + EPS)).T
        o_ref[:, h * HEAD_W:(h + 1) * HEAD_W] = (o * og_ref[...]).astype(bf16)


def _sb_attention(proj, proj_meta, og, *, t):
    return _attn_call(
        _sb_kernel, "sb_attn", [pl.BlockSpec((1, HEAD_W), lambda b, h, i: (0, 0))], (og,),
        proj, proj_meta, _attn_specs(proj.shape[2], t, SB_HEADS_PER_STEP, 3 * N_HEADS, 4 * N_HEADS, 5 * N_HEADS),
        [pltpu.VMEM((SB_HEADS_PER_STEP, META_PAD, t), f32),
         pltpu.VMEM((SB_HEADS_PER_STEP, t, t), f32),
         pltpu.VMEM((SB_HEADS_PER_STEP, t, t), f32)],
        t=t, hps=SB_HEADS_PER_STEP)


def _outproj_kernel(x_ref, md_ref, ms_ref, wd_ref, ws_ref, o_ref):
    o_ref[...] = (x_ref[...]
                  + jnp.dot(md_ref[...], wd_ref[...], preferred_element_type=f32)
                  + jnp.dot(ms_ref[...], ws_ref[...], preferred_element_type=f32))


def _outproj(x2d, mixed_diff, mixed_sb, w_out, *, tm):
    rows, d = x2d.shape
    half = mixed_diff.shape[1]
    return pl.pallas_call(
        _outproj_kernel,
        grid=(rows // tm,),
        in_specs=[
            pl.BlockSpec((tm, d), lambda i: (i, 0)),
            pl.BlockSpec((tm, half), lambda i: (i, 0)),
            pl.BlockSpec((tm, half), lambda i: (i, 0)),
            pl.BlockSpec((half, d), lambda i: (0, 0)),
            pl.BlockSpec((half, d), lambda i: (1, 0)),
        ],
        out_specs=pl.BlockSpec((tm, d), lambda i: (i, 0)),
        out_shape=jax.ShapeDtypeStruct((rows, d), f32),
        compiler_params=pltpu.CompilerParams(
            dimension_semantics=("parallel",), vmem_limit_bytes=VMEM_LIMIT),
        name="outproj",
    )(x2d, mixed_diff, mixed_sb, w_out, w_out)


def _mlp_kernel(h_ref, g_ref, wu_ref, wd_ref, o_ref, m_ref):
    @pl.when(pl.program_id(1) == 0)
    def _():
        h = h_ref[...]
        ms = jnp.mean(h * h, axis=-1, keepdims=True)
        m_ref[...] = (h * lax.rsqrt(ms + EPS) * g_ref[...]).astype(bf16)
        o_ref[...] = h

    hid = jnp.dot(m_ref[...], wu_ref[...], preferred_element_type=f32)
    hid = jnp.square(jnp.maximum(hid, 0.0))
    o_ref[...] += jnp.dot(hid.astype(bf16), wd_ref[...], preferred_element_type=f32)


def _mlp(h1, g_mlp, w_up, w_down, *, tm, tf):
    rows, d = h1.shape
    return pl.pallas_call(
        _mlp_kernel,
        grid=(rows // tm, w_up.shape[1] // tf),
        in_specs=[
            pl.BlockSpec((tm, d), lambda i, f: (i, 0)),
            pl.BlockSpec((1, d), lambda i, f: (0, 0)),
            pl.BlockSpec((d, tf), lambda i, f: (0, f)),
            pl.BlockSpec((tf, d), lambda i, f: (f, 0)),
        ],
        out_specs=pl.BlockSpec((tm, d), lambda i, f: (i, 0)),
        out_shape=jax.ShapeDtypeStruct((rows, d), f32),
        scratch_shapes=[pltpu.VMEM((tm, d), bf16)],
        compiler_params=pltpu.CompilerParams(
            dimension_semantics=("parallel", "arbitrary"), vmem_limit_bytes=VMEM_LIMIT),
        name="mlp",
    )(h1, g_mlp, w_up, w_down)


def _rope_tables(n_pos):
    pos = jnp.arange(n_pos, dtype=f32)
    inv = ROPE_THETA ** (-jnp.arange(0, QK_DIM, 2, dtype=f32) / QK_DIM)
    ang = pos[:, None] * inv[None, :]
    cos, sin = jnp.cos(ang), jnp.sin(ang)
    return jnp.concatenate([cos, cos, cos, cos], axis=-1), jnp.concatenate([-sin, sin, -sin, sin], axis=-1)


def kernel(x, meta_tokens, g_mix, w_in, q_norm_g, k_norm_g, lambda_q1, lambda_k1, lambda_q2, lambda_k2,
           diff_out_g, sb_out_g, w_out, g_mlp, w_up, w_down):
    batch, seq, d = x.shape
    assert g_mix.shape[0] == 1, "single-layer kernel"
    assert meta_tokens.shape[0] == N_META and seq % DIFF_BLOCK == 0 and seq % SB_BLOCK == 0

    x2d = x.reshape(batch * seq, d)
    w_in_b = w_in[0].astype(bf16)
    w_out_b = w_out[0].astype(bf16)
    w_up_b = w_up[0].astype(bf16)
    w_down_b = w_down[0].astype(bf16)
    qg = jnp.tile(q_norm_g[0], 2)[None, :]
    kg = jnp.tile(k_norm_g[0], 2)[None, :]
    cos_t, sin_t = _rope_tables(N_META + seq)

    proj = _inproj(x2d, g_mix, w_in_b, cos_t[N_META:], sin_t[N_META:], qg, kg,
                   batch=batch, seq=seq, tm=256, tn=512)
    proj_meta = _inproj(meta_tokens.astype(f32), g_mix, w_in_b, cos_t[:N_META], sin_t[:N_META], qg, kg,
                        batch=1, seq=N_META, tm=N_META, tn=512)
    proj_meta = jnp.pad(proj_meta, ((0, 0), (0, 0), (0, META_PAD - N_META), (0, 0)))

    mixed_diff = _diff_attention(proj, proj_meta, lambda_q1, lambda_k1, lambda_q2, lambda_k2, diff_out_g,
                                 t=DIFF_BLOCK)
    mixed_sb = _sb_attention(proj, proj_meta, sb_out_g, t=SB_BLOCK)

    h1 = _outproj(x2d, mixed_diff, mixed_sb, w_out_b, tm=512)
    out = _mlp(h1, g_mlp, w_up_b, w_down_b, tm=1024, tf=512)
    return out.reshape(batch, seq, d)
```

```python
import functools
import math

import jax
import jax.numpy as jnp
from jax import lax
from jax.experimental import pallas as pl
from jax.experimental.pallas import tpu as pltpu

N_META = 16
N_HEADS = 8
HEAD_W = 128
QK_DIM = 64
N_GROUPS = 6 * N_HEADS
ROPE_THETA = 10000.0
EPS = 1e-6
NEG_INF = -1e30
LAMBDA_INIT = 0.8 - 0.6 * math.exp(0.0)
LOG2E = math.log2(math.e)

META_PAD = 128
DIFF_BLOCK = 512
SB_BLOCK = 512
SB_CUMSUM_BLOCK = 256
DIFF_HEADS_PER_STEP = 2
SB_HEADS_PER_STEP = 4
VMEM_LIMIT = 56 * 1024 * 1024

f32 = jnp.float32
bf16 = jnp.bfloat16


def _lane_iota(shape):
    return lax.broadcasted_iota(jnp.int32, shape, len(shape) - 1)


def _inproj_kernel(x_ref, g_ref, w_ref, cos_ref, sin_ref, qg_ref, kg_ref, o_ref, *, tn):
    xf = x_ref[...]
    ms = jnp.mean(xf * xf, axis=-1, keepdims=True)
    u = (xf * lax.rsqrt(ms + EPS) * g_ref[...]).astype(bf16)

    ngrp = tn // HEAD_W
    lane = _lane_iota((1, HEAD_W))
    first = lane < QK_DIM
    low_half = (lane & (QK_DIM // 2)) == 0
    cos = cos_ref[...]
    sin = sin_ref[...]

    for j in range(w_ref.shape[1] // tn):
        acc = jnp.dot(u, w_ref[:, j * tn:(j + 1) * tn], preferred_element_type=f32)
        for g in range(ngrp):
            grp = j * ngrp + g
            t = acc[:, g * HEAD_W:(g + 1) * HEAD_W]
            if grp < 2 * N_HEADS:
                is_q = grp < N_HEADS
                ss = t * t
                tot = jnp.sum(ss, axis=-1, keepdims=True)
                lo = jnp.sum(jnp.where(first, ss, 0.0), axis=-1, keepdims=True)
                msq = jnp.where(first, lo, tot - lo) * (1.0 / QK_DIM)
                y = t * lax.rsqrt(msq + EPS) * (qg_ref[...] if is_q else kg_ref[...])
                rot = jnp.where(low_half, pltpu.roll(y, HEAD_W - QK_DIM // 2, 1), pltpu.roll(y, QK_DIM // 2, 1))
                t = y * cos + rot * sin
                if is_q:
                    t = t * (QK_DIM ** -0.5 * LOG2E)
            elif 3 * N_HEADS <= grp < 4 * N_HEADS:
                t = t * (HEAD_W ** -0.5 * LOG2E)
            o_ref[0, grp] = t.astype(bf16)


def _inproj(x2d, g_mix, w_in, cos_t, sin_t, qg, kg, *, batch, seq, tm, tn):
    rows, d = x2d.shape
    nb = seq // tm
    return pl.pallas_call(
        functools.partial(_inproj_kernel, tn=tn),
        grid=(rows // tm,),
        in_specs=[
            pl.BlockSpec((tm, d), lambda i: (i, 0)),
            pl.BlockSpec((1, d), lambda i: (0, 0)),
            pl.BlockSpec(w_in.shape, lambda i: (0, 0), pipeline_mode=pl.Buffered(1)),
            pl.BlockSpec((tm, HEAD_W), lambda i: (i % nb, 0)),
            pl.BlockSpec((tm, HEAD_W), lambda i: (i % nb, 0)),
            pl.BlockSpec((1, HEAD_W), lambda i: (0, 0)),
            pl.BlockSpec((1, HEAD_W), lambda i: (0, 0)),
        ],
        out_specs=pl.BlockSpec((1, N_GROUPS, tm, HEAD_W), lambda i: (i // nb, 0, i % nb, 0)),
        out_shape=jax.ShapeDtypeStruct((batch, N_GROUPS, seq, HEAD_W), bf16),
        compiler_params=pltpu.CompilerParams(dimension_semantics=("parallel",), vmem_limit_bytes=VMEM_LIMIT),
        name="inproj",
    )(x2d, g_mix, w_in, cos_t, sin_t, qg, kg)


def _attn_specs(seq, t, hps, q_grp, k_grp, v_grp):
    return [
        pl.BlockSpec((1, hps, t, HEAD_W), lambda b, h, i: (b, q_grp // hps + h, i, 0)),
        pl.BlockSpec((1, hps, seq, HEAD_W), lambda b, h, i: (b, k_grp // hps + h, 0, 0)),
        pl.BlockSpec((1, hps, seq, HEAD_W), lambda b, h, i: (b, v_grp // hps + h, 0, 0)),
        pl.BlockSpec((1, hps, META_PAD, HEAD_W), lambda b, h, i: (0, k_grp // hps + h, 0, 0)),
        pl.BlockSpec((1, hps, META_PAD, HEAD_W), lambda b, h, i: (0, v_grp // hps + h, 0, 0)),
    ]


def _attn_call(kernel_fn, name, small_specs, small_args, proj, proj_meta, grp_specs, scratch, *, t, hps):
    batch, _, seq, _ = proj.shape
    nq = seq // t
    return pl.pallas_call(
        functools.partial(kernel_fn, t=t, hps=hps),
        grid=(batch, N_HEADS // hps, nq),
        in_specs=small_specs + grp_specs,
        out_specs=pl.BlockSpec((t, hps * HEAD_W), lambda b, h, i: (b * nq + i, h)),
        out_shape=jax.ShapeDtypeStruct((batch * seq, N_HEADS * HEAD_W), bf16),
        scratch_shapes=scratch,
        compiler_params=pltpu.CompilerParams(
            dimension_semantics=("parallel", "parallel", "arbitrary"), vmem_limit_bytes=VMEM_LIMIT),
        name=name,
    )(*small_args, proj, proj, proj, proj_meta, proj_meta)


def _diff_kernel(lq1_ref, lk1_ref, lq2_ref, lk2_ref, og_ref, q_ref, k_ref, v_ref, km_ref, vm_ref, o_ref,
                 s_meta, s_a, s_b, *, t, hps):
    qi = pl.program_id(2)
    heads = range(hps)
    lane_q = _lane_iota((t, HEAD_W))
    nt_dims = (((1,), (1,)), ((), ()))
    tn_dims = (((0,), (0,)), ((), ()))

    def stacked_q(h):
        q = q_ref[0, h]
        zero = jnp.zeros_like(q)
        return jnp.concatenate([jnp.where(lane_q < QK_DIM, q, zero), jnp.where(lane_q >= QK_DIM, q, zero)], axis=0)

    qs = [stacked_q(h) for h in heads]

    lam = (jnp.exp(jnp.sum(lq1_ref[...] * lk1_ref[...], axis=-1, keepdims=True))
           - jnp.exp(jnp.sum(lq2_ref[...] * lk2_ref[...], axis=-1, keepdims=True))
           + LAMBDA_INIT)

    key = lax.broadcasted_iota(jnp.int32, (t, 2 * t), 0)
    qry = lax.broadcasted_iota(jnp.int32, (t, 2 * t), 1)
    causal = key <= jnp.where(qry >= t, qry - t, qry)

    def produce(buf, kblks, mask, m_in):
        m_out = []
        for h in heads:
            s = lax.dot_general(kblks[h], qs[h], nt_dims, preferred_element_type=f32)
            if mask is not None:
                s = jnp.where(mask, s, NEG_INF)
            buf[h] = s
            m_out.append(jnp.maximum(m_in[h], jnp.max(s, axis=0, keepdims=True)))
        return m_out

    def k_block(j):
        start = pl.multiple_of(j * t, t)
        return [k_ref[0, h, pl.ds(start, t), :] for h in heads]

    def consume(buf, vblks, state):
        m_prev, m_cur, l, acc = state
        alphas, ps, ls = [], [], []
        for h in heads:
            alpha = jnp.exp2(m_prev[h] - m_cur[h])
            p = jnp.exp2(buf[h] - m_cur[h])
            ls.append(alpha * l[h] + jnp.sum(p, axis=0, keepdims=True))
            alphas.append(alpha)
            ps.append(p.astype(bf16))
        accs = [alphas[h] * acc[h] + lax.dot_general(vblks[h], ps[h], tn_dims, preferred_element_type=f32)
                for h in heads]
        return ls, accs

    def v_block(j):
        start = pl.multiple_of(j * t, t)
        return [v_ref[0, h, pl.ds(start, t), :] for h in heads]

    def step(cur, nxt, j, state):
        m_next = produce(nxt, k_block(j - 1), None, state[1])
        l, acc = consume(cur, v_block(j), state)
        return (state[1], m_next, l, acc)

    def last_steps(cur, state):
        meta_mask = lax.broadcasted_iota(jnp.int32, (META_PAD, 2 * t), 0) < N_META
        m_meta = produce(s_meta, [km_ref[0, h] for h in heads], meta_mask, state[1])
        l, acc = consume(cur, v_block(0), state)
        return consume(s_meta, [vm_ref[0, h] for h in heads], (state[1], m_meta, l, acc))

    neg = [jnp.full((1, 2 * t), NEG_INF, f32) for _ in heads]
    m_diag = produce(s_a, k_block(qi), causal, neg)
    state = (neg, m_diag, [jnp.zeros((1, 2 * t), f32) for _ in heads],
             [jnp.zeros((HEAD_W, 2 * t), f32) for _ in heads])

    def pair(n, st):
        st = step(s_a, s_b, qi - 2 * n, st)
        return step(s_b, s_a, qi - 2 * n - 1, st)

    state = lax.fori_loop(0, qi // 2, pair, state)

    def odd_tail(st):
        return last_steps(s_b, step(s_a, s_b, 1, st))

    def even_tail(st):
        return last_steps(s_a, st)

    l_fin, acc_fin = lax.cond(qi % 2 == 1, odd_tail, even_tail, state)

    for h in heads:
        o = acc_fin[h] / l_fin[h]
        o = o[:, :t] - lam * o[:, t:]
        ms = jnp.mean(o * o, axis=0, keepdims=True)
        o = (o * lax.rsqrt(ms + EPS)).T
        o_ref[:, h * HEAD_W:(h + 1) * HEAD_W] = (o * og_ref[...] * (1.0 - LAMBDA_INIT)).astype(bf16)


def _diff_attention(proj, proj_meta, lq1, lk1, lq2, lk2, og, *, t):
    vec = lambda n: pl.BlockSpec((1, n), lambda b, h, i: (0, 0))
    return _attn_call(
        _diff_kernel, "diff_attn",
        [vec(QK_DIM), vec(QK_DIM), vec(QK_DIM), vec(QK_DIM), vec(HEAD_W)], (lq1, lk1, lq2, lk2, og),
        proj, proj_meta, _attn_specs(proj.shape[2], t, DIFF_HEADS_PER_STEP, 0, N_HEADS, 2 * N_HEADS),
        [pltpu.VMEM((DIFF_HEADS_PER_STEP, META_PAD, 2 * t), f32),
         pltpu.VMEM((DIFF_HEADS_PER_STEP, t, 2 * t), f32),
         pltpu.VMEM((DIFF_HEADS_PER_STEP, t, 2 * t), f32)],
        t=t, hps=DIFF_HEADS_PER_STEP)


def _sb_kernel(og_ref, q_ref, k_ref, v_ref, km_ref, vm_ref, o_ref, d_meta, d_a, d_b, *, t, hps):
    qi = pl.program_id(2)
    heads = range(hps)
    nt_dims = (((1,), (1,)), ((), ()))
    tn_dims = (((0,), (0,)), ((), ()))
    qh = [q_ref[0, h] for h in heads]

    def later_key_matrix(n):
        r = lax.broadcasted_iota(jnp.int32, (n, n), 0)
        c = lax.broadcasted_iota(jnp.int32, (n, n), 1)
        return (c > r).astype(bf16)

    def scores(kblks, valid):
        zs = [lax.dot_general(kblks[h], qh[h], nt_dims, preferred_element_type=f32) for h in heads]
        if valid is not None:
            zs = [jnp.where(valid, z, NEG_INF) for z in zs]
        return zs

    def produce(buf, zs, lmat):
        nsub = lmat.shape[0]
        subs = range(zs[0].shape[0] // nsub)
        sps, laters = [], []
        for h in heads:
            sp = jnp.maximum(zs[h], 0.0) + jnp.log(1.0 + jnp.exp2(-jnp.abs(zs[h]))) * LOG2E
            buf[h] = zs[h] - sp
            sps.append([sp[i * nsub:i * nsub + 1, :] for i in subs])
            laters.append([jnp.dot(lmat, sp[i * nsub:(i + 1) * nsub].astype(bf16), preferred_element_type=f32)
                           for i in subs])
        totals = []
        for h in heads:
            after = None
            for i in reversed(subs):
                rows = slice(i * nsub, (i + 1) * nsub)
                later = laters[h][i] if after is None else laters[h][i] + after
                buf[h, rows] = buf[h, rows] - later
                sub_total = laters[h][i][0:1, :] + sps[h][i]
                after = sub_total if after is None else after + sub_total
            totals.append(after)
        return totals

    def consume(buf, vblks, run, acc):
        a = [jnp.exp2(buf[h] - run[h]).astype(bf16) for h in heads]
        return [acc[h] + lax.dot_general(vblks[h], a[h], tn_dims, preferred_element_type=f32) for h in heads]

    def k_block(j):
        start = pl.multiple_of(j * t, t)
        return [k_ref[0, h, pl.ds(start, t), :] for h in heads]

    def v_block(j):
        start = pl.multiple_of(j * t, t)
        return [v_ref[0, h, pl.ds(start, t), :] for h in heads]

    lmat_t = later_key_matrix(min(t, SB_CUMSUM_BLOCK))

    def step(cur, nxt, j, state):
        run_c, run_n, acc = state
        zs = scores(k_block(j - 1), None)
        acc = consume(cur, v_block(j), run_c, acc)
        totals = produce(nxt, zs, lmat_t)
        return (run_n, [run_n[h] + totals[h] for h in heads], acc)

    def last_steps(cur, state):
        run_c, run_n, acc = state
        acc = consume(cur, v_block(0), run_c, acc)
        return consume(d_meta, [vm_ref[0, h] for h in heads], run_n, acc)

    key = lax.broadcasted_iota(jnp.int32, (t, t), 0)
    qry = lax.broadcasted_iota(jnp.int32, (t, t), 1)
    zs_meta = scores([km_ref[0, h] for h in heads], lax.broadcasted_iota(jnp.int32, (META_PAD, t), 0) < N_META)
    zs_diag = scores(k_block(qi), key < qry)
    produce(d_meta, zs_meta, later_key_matrix(META_PAD))
    totals = produce(d_a, zs_diag, lmat_t)
    state = ([jnp.zeros((1, t), f32) for _ in heads], totals, [jnp.zeros((HEAD_W, t), f32) for _ in heads])

    def pair(n, st):
        st = step(d_a, d_b, qi - 2 * n, st)
        return step(d_b, d_a, qi - 2 * n - 1, st)

    state = lax.fori_loop(0, qi // 2, pair, state)

    def odd_tail(st):
        return last_steps(d_b, step(d_a, d_b, 1, st))

    def even_tail(st):
        return last_steps(d_a, st)

    acc_fin = lax.cond(qi % 2 == 1, odd_tail, even_tail, state)
    for h in heads:
        acc = acc_fin[h]
        ms = jnp.mean(acc * acc, axis=0, keepdims=True)
        o = (acc * lax.rsqrt(ms + EPS)).T
        o_ref[:, h * HEAD_W:(h + 1) * HEAD_W] = (o * og_ref[...]).astype(bf16)


def _sb_attention(proj, proj_meta, og, *, t):
    return _attn_call(
        _sb_kernel, "sb_attn", [pl.BlockSpec((1, HEAD_W), lambda b, h, i: (0, 0))], (og,),
        proj, proj_meta, _attn_specs(proj.shape[2], t, SB_HEADS_PER_STEP, 3 * N_HEADS, 4 * N_HEADS, 5 * N_HEADS),
        [pltpu.VMEM((SB_HEADS_PER_STEP, META_PAD, t), f32),
         pltpu.VMEM((SB_HEADS_PER_STEP, t, t), f32),
         pltpu.VMEM((SB_HEADS_PER_STEP, t, t), f32)],
        t=t, hps=SB_HEADS_PER_STEP)


def _outproj_kernel(x_ref, md_ref, ms_ref, wd_ref, ws_ref, o_ref):
    o_ref[...] = (x_ref[...]
                  + jnp.dot(md_ref[...], wd_ref[...], preferred_element_type=f32)
                  + jnp.dot(ms_ref[...], ws_ref[...], preferred_element_type=f32))


def _outproj(x2d, mixed_diff, mixed_sb, w_out, *, tm):
    rows, d = x2d.shape
    half = mixed_diff.shape[1]
    return pl.pallas_call(
        _outproj_kernel,
        grid=(rows // tm,),
        in_specs=[
            pl.BlockSpec((tm, d), lambda i: (i, 0)),
            pl.BlockSpec((tm, half), lambda i: (i, 0)),
            pl.BlockSpec((tm, half), lambda i: (i, 0)),
            pl.BlockSpec((half, d), lambda i: (0, 0)),
            pl.BlockSpec((half, d), lambda i: (1, 0)),
        ],
        out_specs=pl.BlockSpec((tm, d), lambda i: (i, 0)),
        out_shape=jax.ShapeDtypeStruct((rows, d), f32),
        compiler_params=pltpu.CompilerParams(
            dimension_semantics=("parallel",), vmem_limit_bytes=VMEM_LIMIT),
        name="outproj",
    )(x2d, mixed_diff, mixed_sb, w_out, w_out)


def _mlp_kernel(h_ref, g_ref, wu_ref, wd_ref, o_ref, m_ref):
    @pl.when(pl.program_id(1) == 0)
    def _():
        h = h_ref[...]
        ms = jnp.mean(h * h, axis=-1, keepdims=True)
        m_ref[...] = (h * lax.rsqrt(ms + EPS) * g_ref[...]).astype(bf16)
        o_ref[...] = h

    hid = jnp.dot(m_ref[...], wu_ref[...], preferred_element_type=f32)
    hid = jnp.square(jnp.maximum(hid, 0.0))
    o_ref[...] += jnp.dot(hid.astype(bf16), wd_ref[...], preferred_element_type=f32)


def _mlp(h1, g_mlp, w_up, w_down, *, tm, tf):
    rows, d = h1.shape
    return pl.pallas_call(
        _mlp_kernel,
        grid=(rows // tm, w_up.shape[1] // tf),
        in_specs=[
            pl.BlockSpec((tm, d), lambda i, f: (i, 0)),
            pl.BlockSpec((1, d), lambda i, f: (0, 0)),
            pl.BlockSpec((d, tf), lambda i, f: (0, f)),
            pl.BlockSpec((tf, d), lambda i, f: (f, 0)),
        ],
        out_specs=pl.BlockSpec((tm, d), lambda i, f: (i, 0)),
        out_shape=jax.ShapeDtypeStruct((rows, d), f32),
        scratch_shapes=[pltpu.VMEM((tm, d), bf16)],
        compiler_params=pltpu.CompilerParams(
            dimension_semantics=("parallel", "arbitrary"), vmem_limit_bytes=VMEM_LIMIT),
        name="mlp",
    )(h1, g_mlp, w_up, w_down)


def _rope_tables(n_pos):
    pos = jnp.arange(n_pos, dtype=f32)
    inv = ROPE_THETA ** (-jnp.arange(0, QK_DIM, 2, dtype=f32) / QK_DIM)
    ang = pos[:, None] * inv[None, :]
    cos, sin = jnp.cos(ang), jnp.sin(ang)
    return jnp.concatenate([cos, cos, cos, cos], axis=-1), jnp.concatenate([-sin, sin, -sin, sin], axis=-1)


def kernel(x, meta_tokens, g_mix, w_in, q_norm_g, k_norm_g, lambda_q1, lambda_k1, lambda_q2, lambda_k2,
           diff_out_g, sb_out_g, w_out, g_mlp, w_up, w_down):
    batch, seq, d = x.shape
    assert g_mix.shape[0] == 1, "single-layer kernel"
    assert meta_tokens.shape[0] == N_META and seq % DIFF_BLOCK == 0 and seq % SB_BLOCK == 0

    x2d = x.reshape(batch * seq, d)
    w_in_b = w_in[0].astype(bf16)
    w_out_b = w_out[0].astype(bf16)
    w_up_b = w_up[0].astype(bf16)
    w_down_b = w_down[0].astype(bf16)
    qg = jnp.tile(q_norm_g[0], 2)[None, :]
    kg = jnp.tile(k_norm_g[0], 2)[None, :]
    cos_t, sin_t = _rope_tables(N_META + seq)

    proj = _inproj(x2d, g_mix, w_in_b, cos_t[N_META:], sin_t[N_META:], qg, kg,
                   batch=batch, seq=seq, tm=256, tn=512)
    proj_meta = _inproj(meta_tokens.astype(f32), g_mix, w_in_b, cos_t[:N_META], sin_t[:N_META], qg, kg,
                        batch=1, seq=N_META, tm=N_META, tn=512)
    proj_meta = jnp.pad(proj_meta, ((0, 0), (0, 0), (0, META_PAD - N_META), (0, 0)))

    mixed_diff = _diff_attention(proj, proj_meta, lambda_q1, lambda_k1, lambda_q2, lambda_k2, diff_out_g,
                                 t=DIFF_BLOCK)
    mixed_sb = _sb_attention(proj, proj_meta, sb_out_g, t=SB_BLOCK)

    h1 = _outproj(x2d, mixed_diff, mixed_sb, w_out_b, tm=512)
    out = _mlp(h1, g_mlp, w_up_b, w_down_b, tm=512, tf=1024)
    return out.reshape(batch, seq, d)
```

```python
import functools
import math

import jax
import jax.numpy as jnp
from jax import lax
from jax.experimental import pallas as pl
from jax.experimental.pallas import tpu as pltpu

N_META = 16
N_HEADS = 8
HEAD_W = 128
QK_DIM = 64
N_GROUPS = 6 * N_HEADS
ROPE_THETA = 10000.0
EPS = 1e-6
NEG_INF = -1e30
LAMBDA_INIT = 0.8 - 0.6 * math.exp(0.0)
LOG2E = math.log2(math.e)

META_PAD = 128
DIFF_BLOCK = 512
SB_BLOCK = 256
SB_CUMSUM_BLOCK = 256
SB_DEAD_LOG2 = 160.0
DIFF_HEADS_PER_STEP = 2
SB_HEADS_PER_STEP = 4
VMEM_LIMIT = 56 * 1024 * 1024

f32 = jnp.float32
bf16 = jnp.bfloat16


def _lane_iota(shape):
    return lax.broadcasted_iota(jnp.int32, shape, len(shape) - 1)


def _inproj_kernel(x_ref, g_ref, w_ref, cos_ref, sin_ref, qg_ref, kg_ref, o_ref, *, tn):
    xf = x_ref[...]
    ms = jnp.mean(xf * xf, axis=-1, keepdims=True)
    u = (xf * lax.rsqrt(ms + EPS) * g_ref[...]).astype(bf16)

    ngrp = tn // HEAD_W
    lane = _lane_iota((1, HEAD_W))
    first = lane < QK_DIM
    low_half = (lane & (QK_DIM // 2)) == 0
    cos = cos_ref[...]
    sin = sin_ref[...]

    for j in range(w_ref.shape[1] // tn):
        acc = jnp.dot(u, w_ref[:, j * tn:(j + 1) * tn], preferred_element_type=f32)
        for g in range(ngrp):
            grp = j * ngrp + g
            t = acc[:, g * HEAD_W:(g + 1) * HEAD_W]
            if grp < 2 * N_HEADS:
                is_q = grp < N_HEADS
                ss = t * t
                tot = jnp.sum(ss, axis=-1, keepdims=True)
                lo = jnp.sum(jnp.where(first, ss, 0.0), axis=-1, keepdims=True)
                msq = jnp.where(first, lo, tot - lo) * (1.0 / QK_DIM)
                y = t * lax.rsqrt(msq + EPS) * (qg_ref[...] if is_q else kg_ref[...])
                rot = jnp.where(low_half, pltpu.roll(y, HEAD_W - QK_DIM // 2, 1), pltpu.roll(y, QK_DIM // 2, 1))
                t = y * cos + rot * sin
                if is_q:
                    t = t * (QK_DIM ** -0.5 * LOG2E)
            elif 3 * N_HEADS <= grp < 4 * N_HEADS:
                t = t * (HEAD_W ** -0.5 * LOG2E)
            o_ref[0, grp] = t.astype(bf16)


def _inproj(x2d, g_mix, w_in, cos_t, sin_t, qg, kg, *, batch, seq, tm, tn):
    rows, d = x2d.shape
    nb = seq // tm
    return pl.pallas_call(
        functools.partial(_inproj_kernel, tn=tn),
        grid=(rows // tm,),
        in_specs=[
            pl.BlockSpec((tm, d), lambda i: (i, 0)),
            pl.BlockSpec((1, d), lambda i: (0, 0)),
            pl.BlockSpec(w_in.shape, lambda i: (0, 0), pipeline_mode=pl.Buffered(1)),
            pl.BlockSpec((tm, HEAD_W), lambda i: (i % nb, 0)),
            pl.BlockSpec((tm, HEAD_W), lambda i: (i % nb, 0)),
            pl.BlockSpec((1, HEAD_W), lambda i: (0, 0)),
            pl.BlockSpec((1, HEAD_W), lambda i: (0, 0)),
        ],
        out_specs=pl.BlockSpec((1, N_GROUPS, tm, HEAD_W), lambda i: (i // nb, 0, i % nb, 0)),
        out_shape=jax.ShapeDtypeStruct((batch, N_GROUPS, seq, HEAD_W), bf16),
        compiler_params=pltpu.CompilerParams(dimension_semantics=("parallel",), vmem_limit_bytes=VMEM_LIMIT),
        name="inproj",
    )(x2d, g_mix, w_in, cos_t, sin_t, qg, kg)


def _attn_specs(seq, t, hps, q_grp, k_grp, v_grp):
    return [
        pl.BlockSpec((1, hps, t, HEAD_W), lambda b, h, i: (b, q_grp // hps + h, i, 0)),
        pl.BlockSpec((1, hps, seq, HEAD_W), lambda b, h, i: (b, k_grp // hps + h, 0, 0)),
        pl.BlockSpec((1, hps, seq, HEAD_W), lambda b, h, i: (b, v_grp // hps + h, 0, 0)),
        pl.BlockSpec((1, hps, META_PAD, HEAD_W), lambda b, h, i: (0, k_grp // hps + h, 0, 0)),
        pl.BlockSpec((1, hps, META_PAD, HEAD_W), lambda b, h, i: (0, v_grp // hps + h, 0, 0)),
    ]


def _attn_call(kernel_fn, name, small_specs, small_args, proj, proj_meta, grp_specs, scratch, *, t, hps):
    batch, _, seq, _ = proj.shape
    nq = seq // t
    return pl.pallas_call(
        functools.partial(kernel_fn, t=t, hps=hps),
        grid=(batch, N_HEADS // hps, nq),
        in_specs=small_specs + grp_specs,
        out_specs=pl.BlockSpec((t, hps * HEAD_W), lambda b, h, i: (b * nq + i, h)),
        out_shape=jax.ShapeDtypeStruct((batch * seq, N_HEADS * HEAD_W), bf16),
        scratch_shapes=scratch,
        compiler_params=pltpu.CompilerParams(
            dimension_semantics=("parallel", "parallel", "arbitrary"), vmem_limit_bytes=VMEM_LIMIT),
        name=name,
    )(*small_args, proj, proj, proj, proj_meta, proj_meta)


def _diff_kernel(lq1_ref, lk1_ref, lq2_ref, lk2_ref, og_ref, q_ref, k_ref, v_ref, km_ref, vm_ref, o_ref,
                 s_meta, s_a, s_b, *, t, hps):
    qi = pl.program_id(2)
    heads = range(hps)
    lane_q = _lane_iota((t, HEAD_W))
    nt_dims = (((1,), (1,)), ((), ()))
    tn_dims = (((0,), (0,)), ((), ()))

    def stacked_q(h):
        q = q_ref[0, h]
        zero = jnp.zeros_like(q)
        return jnp.concatenate([jnp.where(lane_q < QK_DIM, q, zero), jnp.where(lane_q >= QK_DIM, q, zero)], axis=0)

    qs = [stacked_q(h) for h in heads]

    lam = (jnp.exp(jnp.sum(lq1_ref[...] * lk1_ref[...], axis=-1, keepdims=True))
           - jnp.exp(jnp.sum(lq2_ref[...] * lk2_ref[...], axis=-1, keepdims=True))
           + LAMBDA_INIT)

    key = lax.broadcasted_iota(jnp.int32, (t, 2 * t), 0)
    qry = lax.broadcasted_iota(jnp.int32, (t, 2 * t), 1)
    causal = key <= jnp.where(qry >= t, qry - t, qry)

    def produce(buf, kblks, mask, m_in):
        m_out = []
        for h in heads:
            s = lax.dot_general(kblks[h], qs[h], nt_dims, preferred_element_type=f32)
            if mask is not None:
                s = jnp.where(mask, s, NEG_INF)
            buf[h] = s
            m_out.append(jnp.maximum(m_in[h], jnp.max(s, axis=0, keepdims=True)))
        return m_out

    def k_block(j):
        start = pl.multiple_of(j * t, t)
        return [k_ref[0, h, pl.ds(start, t), :] for h in heads]

    def consume(buf, vblks, state):
        m_prev, m_cur, l, acc = state
        alphas, ps, ls = [], [], []
        for h in heads:
            alpha = jnp.exp2(m_prev[h] - m_cur[h])
            p = jnp.exp2(buf[h] - m_cur[h])
            ls.append(alpha * l[h] + jnp.sum(p, axis=0, keepdims=True))
            alphas.append(alpha)
            ps.append(p.astype(bf16))
        accs = [alphas[h] * acc[h] + lax.dot_general(vblks[h], ps[h], tn_dims, preferred_element_type=f32)
                for h in heads]
        return ls, accs

    def v_block(j):
        start = pl.multiple_of(j * t, t)
        return [v_ref[0, h, pl.ds(start, t), :] for h in heads]

    def step(cur, nxt, j, state):
        m_next = produce(nxt, k_block(j - 1), None, state[1])
        l, acc = consume(cur, v_block(j), state)
        return (state[1], m_next, l, acc)

    def last_steps(cur, state):
        meta_mask = lax.broadcasted_iota(jnp.int32, (META_PAD, 2 * t), 0) < N_META
        m_meta = produce(s_meta, [km_ref[0, h] for h in heads], meta_mask, state[1])
        l, acc = consume(cur, v_block(0), state)
        return consume(s_meta, [vm_ref[0, h] for h in heads], (state[1], m_meta, l, acc))

    neg = [jnp.full((1, 2 * t), NEG_INF, f32) for _ in heads]
    m_diag = produce(s_a, k_block(qi), causal, neg)
    state = (neg, m_diag, [jnp.zeros((1, 2 * t), f32) for _ in heads],
             [jnp.zeros((HEAD_W, 2 * t), f32) for _ in heads])

    def pair(n, st):
        st = step(s_a, s_b, qi - 2 * n, st)
        return step(s_b, s_a, qi - 2 * n - 1, st)

    state = lax.fori_loop(0, qi // 2, pair, state)

    def odd_tail(st):
        return last_steps(s_b, step(s_a, s_b, 1, st))

    def even_tail(st):
        return last_steps(s_a, st)

    l_fin, acc_fin = lax.cond(qi % 2 == 1, odd_tail, even_tail, state)

    for h in heads:
        o = acc_fin[h] / l_fin[h]
        o = o[:, :t] - lam * o[:, t:]
        ms = jnp.mean(o * o, axis=0, keepdims=True)
        o = (o * lax.rsqrt(ms + EPS)).T
        o_ref[:, h * HEAD_W:(h + 1) * HEAD_W] = (o * og_ref[...] * (1.0 - LAMBDA_INIT)).astype(bf16)


def _diff_attention(proj, proj_meta, lq1, lk1, lq2, lk2, og, *, t):
    vec = lambda n: pl.BlockSpec((1, n), lambda b, h, i: (0, 0))
    return _attn_call(
        _diff_kernel, "diff_attn",
        [vec(QK_DIM), vec(QK_DIM), vec(QK_DIM), vec(QK_DIM), vec(HEAD_W)], (lq1, lk1, lq2, lk2, og),
        proj, proj_meta, _attn_specs(proj.shape[2], t, DIFF_HEADS_PER_STEP, 0, N_HEADS, 2 * N_HEADS),
        [pltpu.VMEM((DIFF_HEADS_PER_STEP, META_PAD, 2 * t), f32),
         pltpu.VMEM((DIFF_HEADS_PER_STEP, t, 2 * t), f32),
         pltpu.VMEM((DIFF_HEADS_PER_STEP, t, 2 * t), f32)],
        t=t, hps=DIFF_HEADS_PER_STEP)


def _sb_kernel(og_ref, q_ref, k_ref, v_ref, km_ref, vm_ref, o_ref, d_meta, d_a, d_b, *, t, hps):
    qi = pl.program_id(2)
    heads = range(hps)
    nt_dims = (((1,), (1,)), ((), ()))
    tn_dims = (((0,), (0,)), ((), ()))
    qh = [q_ref[0, h] for h in heads]

    def later_key_matrix(n):
        r = lax.broadcasted_iota(jnp.int32, (n, n), 0)
        c = lax.broadcasted_iota(jnp.int32, (n, n), 1)
        return (c > r).astype(bf16)

    def scores(kblks, valid):
        zs = [lax.dot_general(kblks[h], qh[h], nt_dims, preferred_element_type=f32) for h in heads]
        if valid is not None:
            zs = [jnp.where(valid, z, NEG_INF) for z in zs]
        return zs

    def produce(buf, zs, lmat):
        nsub = lmat.shape[0]
        subs = range(zs[0].shape[0] // nsub)
        sps, laters = [], []
        for h in heads:
            sp = jnp.maximum(zs[h], 0.0) + jnp.log(1.0 + jnp.exp2(-jnp.abs(zs[h]))) * LOG2E
            buf[h] = zs[h] - sp
            sps.append([sp[i * nsub:i * nsub + 1, :] for i in subs])
            laters.append([jnp.dot(lmat, sp[i * nsub:(i + 1) * nsub].astype(bf16), preferred_element_type=f32)
                           for i in subs])
        totals = []
        for h in heads:
            after = None
            for i in reversed(subs):
                rows = slice(i * nsub, (i + 1) * nsub)
                later = laters[h][i] if after is None else laters[h][i] + after
                buf[h, rows] = buf[h, rows] - later
                sub_total = laters[h][i][0:1, :] + sps[h][i]
                after = sub_total if after is None else after + sub_total
            totals.append(after)
        return totals

    def consume(buf, vblks, run, acc):
        a = [jnp.exp2(buf[h] - run[h]).astype(bf16) for h in heads]
        return [acc[h] + lax.dot_general(vblks[h], a[h], tn_dims, preferred_element_type=f32) for h in heads]

    def k_block(j):
        start = pl.multiple_of(j * t, t)
        return [k_ref[0, h, pl.ds(start, t), :] for h in heads]

    def v_block(j):
        start = pl.multiple_of(j * t, t)
        return [v_ref[0, h, pl.ds(start, t), :] for h in heads]

    lmat_t = later_key_matrix(min(t, SB_CUMSUM_BLOCK))

    def step(cur, nxt, j, state):
        run_c, run_n, acc = state
        zs = scores(k_block(j - 1), None)
        acc = consume(cur, v_block(j), run_c, acc)
        totals = produce(nxt, zs, lmat_t)
        return (run_n, [run_n[h] + totals[h] for h in heads], acc)

    def last_steps(cur, state):
        run_c, run_n, acc = state
        acc = consume(cur, v_block(0), run_c, acc)
        return consume(d_meta, [vm_ref[0, h] for h in heads], run_n, acc)

    key = lax.broadcasted_iota(jnp.int32, (t, t), 0)
    qry = lax.broadcasted_iota(jnp.int32, (t, t), 1)
    zs_meta = scores([km_ref[0, h] for h in heads], lax.broadcasted_iota(jnp.int32, (META_PAD, t), 0) < N_META)
    zs_diag = scores(k_block(qi), key < qry)
    produce(d_meta, zs_meta, later_key_matrix(META_PAD))
    totals = produce(d_a, zs_diag, lmat_t)
    state = ([jnp.zeros((1, t), f32) for _ in heads], totals, [jnp.zeros((HEAD_W, t), f32) for _ in heads])

    def pair(n, st):
        st = step(d_a, d_b, qi - 2 * n, st)
        return step(d_b, d_a, qi - 2 * n - 1, st)

    def alive(st):
        lowest = st[0][0]
        for h in heads[1:]:
            lowest = jnp.minimum(lowest, st[0][h])
        return jnp.min(lowest) < SB_DEAD_LOG2

    def pair_while(carry):
        n, _, st = carry
        st = pair(n, st)
        return n + 1, alive(st), st

    _, still_alive, state = lax.while_loop(
        lambda carry: jnp.logical_and(carry[0] < qi // 2, carry[1]), pair_while, (jnp.int32(0), True, state))

    def odd_tail(st):
        return last_steps(d_b, step(d_a, d_b, 1, st))

    def even_tail(st):
        return last_steps(d_a, st)

    def remaining(st):
        return lax.cond(qi % 2 == 1, odd_tail, even_tail, st)

    acc_fin = lax.cond(still_alive, remaining, lambda st: st[2], state)
    for h in heads:
        acc = acc_fin[h]
        ms = jnp.mean(acc * acc, axis=0, keepdims=True)
        o = (acc * lax.rsqrt(ms + EPS)).T
        o_ref[:, h * HEAD_W:(h + 1) * HEAD_W] = (o * og_ref[...]).astype(bf16)


def _sb_attention(proj, proj_meta, og, *, t):
    return _attn_call(
        _sb_kernel, "sb_attn", [pl.BlockSpec((1, HEAD_W), lambda b, h, i: (0, 0))], (og,),
        proj, proj_meta, _attn_specs(proj.shape[2], t, SB_HEADS_PER_STEP, 3 * N_HEADS, 4 * N_HEADS, 5 * N_HEADS),
        [pltpu.VMEM((SB_HEADS_PER_STEP, META_PAD, t), f32),
         pltpu.VMEM((SB_HEADS_PER_STEP, t, t), f32),
         pltpu.VMEM((SB_HEADS_PER_STEP, t, t), f32)],
        t=t, hps=SB_HEADS_PER_STEP)


def _outproj_kernel(x_ref, md_ref, ms_ref, wd_ref, ws_ref, o_ref):
    o_ref[...] = (x_ref[...]
                  + jnp.dot(md_ref[...], wd_ref[...], preferred_element_type=f32)
                  + jnp.dot(ms_ref[...], ws_ref[...], preferred_element_type=f32))


def _outproj(x2d, mixed_diff, mixed_sb, w_out, *, tm):
    rows, d = x2d.shape
    half = mixed_diff.shape[1]
    return pl.pallas_call(
        _outproj_kernel,
        grid=(rows // tm,),
        in_specs=[
            pl.BlockSpec((tm, d), lambda i: (i, 0)),
            pl.BlockSpec((tm, half), lambda i: (i, 0)),
            pl.BlockSpec((tm, half), lambda i: (i, 0)),
            pl.BlockSpec((half, d), lambda i: (0, 0)),
            pl.BlockSpec((half, d), lambda i: (1, 0)),
        ],
        out_specs=pl.BlockSpec((tm, d), lambda i: (i, 0)),
        out_shape=jax.ShapeDtypeStruct((rows, d), f32),
        compiler_params=pltpu.CompilerParams(
            dimension_semantics=("parallel",), vmem_limit_bytes=VMEM_LIMIT),
        name="outproj",
    )(x2d, mixed_diff, mixed_sb, w_out, w_out)


def _mlp_kernel(h_ref, g_ref, wu_ref, wd_ref, o_ref, m_ref):
    @pl.when(pl.program_id(1) == 0)
    def _():
        h = h_ref[...]
        ms = jnp.mean(h * h, axis=-1, keepdims=True)
        m_ref[...] = (h * lax.rsqrt(ms + EPS) * g_ref[...]).astype(bf16)
        o_ref[...] = h

    hid = jnp.dot(m_ref[...], wu_ref[...], preferred_element_type=f32)
    hid = jnp.square(jnp.maximum(hid, 0.0))
    o_ref[...] += jnp.dot(hid.astype(bf16), wd_ref[...], preferred_element_type=f32)


def _mlp(h1, g_mlp, w_up, w_down, *, tm, tf):
    rows, d = h1.shape
    return pl.pallas_call(
        _mlp_kernel,
        grid=(rows // tm, w_up.shape[1] // tf),
        in_specs=[
            pl.BlockSpec((tm, d), lambda i, f: (i, 0)),
            pl.BlockSpec((1, d), lambda i, f: (0, 0)),
            pl.BlockSpec((d, tf), lambda i, f: (0, f)),
            pl.BlockSpec((tf, d), lambda i, f: (f, 0)),
        ],
        out_specs=pl.BlockSpec((tm, d), lambda i, f: (i, 0)),
        out_shape=jax.ShapeDtypeStruct((rows, d), f32),
        scratch_shapes=[pltpu.VMEM((tm, d), bf16)],
        compiler_params=pltpu.CompilerParams(
            dimension_semantics=("parallel", "arbitrary"), vmem_limit_bytes=VMEM_LIMIT),
        name="mlp",
    )(h1, g_mlp, w_up, w_down)


def _rope_tables(n_pos):
    pos = jnp.arange(n_pos, dtype=f32)
    inv = ROPE_THETA ** (-jnp.arange(0, QK_DIM, 2, dtype=f32) / QK_DIM)
    ang = pos[:, None] * inv[None, :]
    cos, sin = jnp.cos(ang), jnp.sin(ang)
    return jnp.concatenate([cos, cos, cos, cos], axis=-1), jnp.concatenate([-sin, sin, -sin, sin], axis=-1)


def kernel(x, meta_tokens, g_mix, w_in, q_norm_g, k_norm_g, lambda_q1, lambda_k1, lambda_q2, lambda_k2,
           diff_out_g, sb_out_g, w_out, g_mlp, w_up, w_down):
    batch, seq, d = x.shape
    assert g_mix.shape[0] == 1, "single-layer kernel"
    assert meta_tokens.shape[0] == N_META and seq % DIFF_BLOCK == 0 and seq % SB_BLOCK == 0

    x2d = x.reshape(batch * seq, d)
    w_in_b = w_in[0].astype(bf16)
    w_out_b = w_out[0].astype(bf16)
    w_up_b = w_up[0].astype(bf16)
    w_down_b = w_down[0].astype(bf16)
    qg = jnp.tile(q_norm_g[0], 2)[None, :]
    kg = jnp.tile(k_norm_g[0], 2)[None, :]
    cos_t, sin_t = _rope_tables(N_META + seq)

    proj = _inproj(x2d, g_mix, w_in_b, cos_t[N_META:], sin_t[N_META:], qg, kg,
                   batch=batch, seq=seq, tm=256, tn=512)
    proj_meta = _inproj(meta_tokens.astype(f32), g_mix, w_in_b, cos_t[:N_META], sin_t[:N_META], qg, kg,
                        batch=1, seq=N_META, tm=N_META, tn=512)
    proj_meta = jnp.pad(proj_meta, ((0, 0), (0, 0), (0, META_PAD - N_META), (0, 0)))

    mixed_diff = _diff_attention(proj, proj_meta, lambda_q1, lambda_k1, lambda_q2, lambda_k2, diff_out_g,
                                 t=DIFF_BLOCK)
    mixed_sb = _sb_attention(proj, proj_meta, sb_out_g, t=SB_BLOCK)

    h1 = _outproj(x2d, mixed_diff, mixed_sb, w_out_b, tm=512)
    out = _mlp(h1, g_mlp, w_up_b, w_down_b, tm=512, tf=1024)
    return out.reshape(batch, seq, d)
```

```python
import functools
import math

import jax
import jax.numpy as jnp
from jax import lax
from jax.experimental import pallas as pl
from jax.experimental.pallas import tpu as pltpu

N_META = 16
N_HEADS = 8
HEAD_W = 128
QK_DIM = 64
N_GROUPS = 6 * N_HEADS
ROPE_THETA = 10000.0
EPS = 1e-6
NEG_INF = -1e30
LAMBDA_INIT = 0.8 - 0.6 * math.exp(0.0)
LOG2E = math.log2(math.e)

INPROJ_ROWS, INPROJ_COLS = 256, 512
OUTPROJ_ROWS = 512
MLP_ROWS, MLP_HIDDEN_COLS = 512, 1024
META_PAD = 128
DIFF_BLOCK = 512
SB_BLOCK = 256
SB_CUMSUM_BLOCK = 256
SB_DEAD_LOG2 = 160.0
DIFF_HEADS_PER_STEP = 4
SB_HEADS_PER_STEP = 4
VMEM_LIMIT = 56 * 1024 * 1024

f32 = jnp.float32
bf16 = jnp.bfloat16


def _lane_iota(shape):
    return lax.broadcasted_iota(jnp.int32, shape, len(shape) - 1)


def _inproj_kernel(x_ref, g_ref, w_ref, cos_ref, sin_ref, qg_ref, kg_ref, o_ref, *, tn):
    xf = x_ref[...]
    ms = jnp.mean(xf * xf, axis=-1, keepdims=True)
    u = (xf * lax.rsqrt(ms + EPS) * g_ref[...]).astype(bf16)

    ngrp = tn // HEAD_W
    lane = _lane_iota((1, HEAD_W))
    first = lane < QK_DIM
    low_half = (lane & (QK_DIM // 2)) == 0
    cos = cos_ref[...]
    sin = sin_ref[...]

    for j in range(w_ref.shape[1] // tn):
        acc = jnp.dot(u, w_ref[:, j * tn:(j + 1) * tn], preferred_element_type=f32)
        for g in range(ngrp):
            grp = j * ngrp + g
            t = acc[:, g * HEAD_W:(g + 1) * HEAD_W]
            if grp < 2 * N_HEADS:
                is_q = grp < N_HEADS
                ss = t * t
                tot = jnp.sum(ss, axis=-1, keepdims=True)
                lo = jnp.sum(jnp.where(first, ss, 0.0), axis=-1, keepdims=True)
                msq = jnp.where(first, lo, tot - lo) * (1.0 / QK_DIM)
                y = t * lax.rsqrt(msq + EPS) * (qg_ref[...] if is_q else kg_ref[...])
                rot = jnp.where(low_half, pltpu.roll(y, HEAD_W - QK_DIM // 2, 1), pltpu.roll(y, QK_DIM // 2, 1))
                t = y * cos + rot * sin
                if is_q:
                    t = t * (QK_DIM ** -0.5 * LOG2E)
            elif 3 * N_HEADS <= grp < 4 * N_HEADS:
                t = t * (HEAD_W ** -0.5 * LOG2E)
            o_ref[0, grp] = t.astype(bf16)


def _inproj(x2d, g_mix, w_in, cos_t, sin_t, qg, kg, *, batch, seq, tm, tn):
    rows, d = x2d.shape
    nb = seq // tm
    return pl.pallas_call(
        functools.partial(_inproj_kernel, tn=tn),
        grid=(rows // tm,),
        in_specs=[
            pl.BlockSpec((tm, d), lambda i: (i, 0)),
            pl.BlockSpec((1, d), lambda i: (0, 0)),
            pl.BlockSpec(w_in.shape, lambda i: (0, 0), pipeline_mode=pl.Buffered(1)),
            pl.BlockSpec((tm, HEAD_W), lambda i: (i % nb, 0)),
            pl.BlockSpec((tm, HEAD_W), lambda i: (i % nb, 0)),
            pl.BlockSpec((1, HEAD_W), lambda i: (0, 0)),
            pl.BlockSpec((1, HEAD_W), lambda i: (0, 0)),
        ],
        out_specs=pl.BlockSpec((1, N_GROUPS, tm, HEAD_W), lambda i: (i // nb, 0, i % nb, 0)),
        out_shape=jax.ShapeDtypeStruct((batch, N_GROUPS, seq, HEAD_W), bf16),
        compiler_params=pltpu.CompilerParams(dimension_semantics=("parallel",), vmem_limit_bytes=VMEM_LIMIT),
        name="inproj",
    )(x2d, g_mix, w_in, cos_t, sin_t, qg, kg)


def _attn_specs(seq, t, hps, q_grp, k_grp, v_grp):
    return [
        pl.BlockSpec((1, hps, t, HEAD_W), lambda b, h, i: (b, q_grp // hps + h, i, 0)),
        pl.BlockSpec((1, hps, seq, HEAD_W), lambda b, h, i: (b, k_grp // hps + h, 0, 0)),
        pl.BlockSpec((1, hps, seq, HEAD_W), lambda b, h, i: (b, v_grp // hps + h, 0, 0)),
        pl.BlockSpec((1, hps, META_PAD, HEAD_W), lambda b, h, i: (0, k_grp // hps + h, 0, 0)),
        pl.BlockSpec((1, hps, META_PAD, HEAD_W), lambda b, h, i: (0, v_grp // hps + h, 0, 0)),
    ]


def _attn_call(kernel_fn, name, small_specs, small_args, proj, proj_meta, grp_specs, scratch, *, t, hps):
    batch, _, seq, _ = proj.shape
    nq = seq // t
    return pl.pallas_call(
        functools.partial(kernel_fn, t=t, hps=hps),
        grid=(batch, N_HEADS // hps, nq),
        in_specs=small_specs + grp_specs,
        out_specs=pl.BlockSpec((t, hps * HEAD_W), lambda b, h, i: (b * nq + i, h)),
        out_shape=jax.ShapeDtypeStruct((batch * seq, N_HEADS * HEAD_W), bf16),
        scratch_shapes=scratch,
        compiler_params=pltpu.CompilerParams(
            dimension_semantics=("parallel", "parallel", "arbitrary"), vmem_limit_bytes=VMEM_LIMIT),
        name=name,
    )(*small_args, proj, proj, proj, proj_meta, proj_meta)


def _diff_kernel(lq1_ref, lk1_ref, lq2_ref, lk2_ref, og_ref, q_ref, k_ref, v_ref, km_ref, vm_ref, o_ref,
                 s_meta, s_a, s_b, *, t, hps):
    qi = pl.program_id(2)
    heads = range(hps)
    lane_q = _lane_iota((t, HEAD_W))
    nt_dims = (((1,), (1,)), ((), ()))
    tn_dims = (((0,), (0,)), ((), ()))

    def stacked_q(h):
        q = q_ref[0, h]
        zero = jnp.zeros_like(q)
        return jnp.concatenate([jnp.where(lane_q < QK_DIM, q, zero), jnp.where(lane_q >= QK_DIM, q, zero)], axis=0)

    qs = [stacked_q(h) for h in heads]

    lam = (jnp.exp(jnp.sum(lq1_ref[...] * lk1_ref[...], axis=-1, keepdims=True))
           - jnp.exp(jnp.sum(lq2_ref[...] * lk2_ref[...], axis=-1, keepdims=True))
           + LAMBDA_INIT)

    key = lax.broadcasted_iota(jnp.int32, (t, 2 * t), 0)
    qry = lax.broadcasted_iota(jnp.int32, (t, 2 * t), 1)
    causal = key <= jnp.where(qry >= t, qry - t, qry)

    def produce(buf, kblks, mask, m_in):
        m_out = []
        for h in heads:
            s = lax.dot_general(kblks[h], qs[h], nt_dims, preferred_element_type=f32)
            if mask is not None:
                s = jnp.where(mask, s, NEG_INF)
            buf[h] = s
            m_out.append(jnp.maximum(m_in[h], jnp.max(s, axis=0, keepdims=True)))
        return m_out

    def k_block(j):
        start = pl.multiple_of(j * t, t)
        return [k_ref[0, h, pl.ds(start, t), :] for h in heads]

    def consume(buf, vblks, state):
        m_prev, m_cur, l, acc = state
        alphas, ps, ls = [], [], []
        for h in heads:
            alpha = jnp.exp2(m_prev[h] - m_cur[h])
            p = jnp.exp2(buf[h] - m_cur[h])
            ls.append(alpha * l[h] + jnp.sum(p, axis=0, keepdims=True))
            alphas.append(alpha)
            ps.append(p.astype(bf16))
        accs = [alphas[h] * acc[h] + lax.dot_general(vblks[h], ps[h], tn_dims, preferred_element_type=f32)
                for h in heads]
        return ls, accs

    def v_block(j):
        start = pl.multiple_of(j * t, t)
        return [v_ref[0, h, pl.ds(start, t), :] for h in heads]

    def step(cur, nxt, j, state):
        m_next = produce(nxt, k_block(j - 1), None, state[1])
        l, acc = consume(cur, v_block(j), state)
        return (state[1], m_next, l, acc)

    def last_steps(cur, state):
        meta_mask = lax.broadcasted_iota(jnp.int32, (META_PAD, 2 * t), 0) < N_META
        m_meta = produce(s_meta, [km_ref[0, h] for h in heads], meta_mask, state[1])
        l, acc = consume(cur, v_block(0), state)
        return consume(s_meta, [vm_ref[0, h] for h in heads], (state[1], m_meta, l, acc))

    neg = [jnp.full((1, 2 * t), NEG_INF, f32) for _ in heads]
    m_diag = produce(s_a, k_block(qi), causal, neg)
    state = (neg, m_diag, [jnp.zeros((1, 2 * t), f32) for _ in heads],
             [jnp.zeros((HEAD_W, 2 * t), f32) for _ in heads])

    def pair(n, st):
        st = step(s_a, s_b, qi - 2 * n, st)
        return step(s_b, s_a, qi - 2 * n - 1, st)

    state = lax.fori_loop(0, qi // 2, pair, state)

    def odd_tail(st):
        return last_steps(s_b, step(s_a, s_b, 1, st))

    def even_tail(st):
        return last_steps(s_a, st)

    l_fin, acc_fin = lax.cond(qi % 2 == 1, odd_tail, even_tail, state)

    for h in heads:
        o = acc_fin[h] / l_fin[h]
        o = o[:, :t] - lam * o[:, t:]
        ms = jnp.mean(o * o, axis=0, keepdims=True)
        o = (o * lax.rsqrt(ms + EPS)).T
        o_ref[:, h * HEAD_W:(h + 1) * HEAD_W] = (o * og_ref[...] * (1.0 - LAMBDA_INIT)).astype(bf16)


def _diff_attention(proj, proj_meta, lq1, lk1, lq2, lk2, og, *, t):
    vec = lambda n: pl.BlockSpec((1, n), lambda b, h, i: (0, 0))
    return _attn_call(
        _diff_kernel, "diff_attn",
        [vec(QK_DIM), vec(QK_DIM), vec(QK_DIM), vec(QK_DIM), vec(HEAD_W)], (lq1, lk1, lq2, lk2, og),
        proj, proj_meta, _attn_specs(proj.shape[2], t, DIFF_HEADS_PER_STEP, 0, N_HEADS, 2 * N_HEADS),
        [pltpu.VMEM((DIFF_HEADS_PER_STEP, META_PAD, 2 * t), f32),
         pltpu.VMEM((DIFF_HEADS_PER_STEP, t, 2 * t), f32),
         pltpu.VMEM((DIFF_HEADS_PER_STEP, t, 2 * t), f32)],
        t=t, hps=DIFF_HEADS_PER_STEP)


def _sb_kernel(og_ref, q_ref, k_ref, v_ref, km_ref, vm_ref, o_ref, d_meta, d_a, d_b, *, t, hps):
    qi = pl.program_id(2)
    heads = range(hps)
    nt_dims = (((1,), (1,)), ((), ()))
    tn_dims = (((0,), (0,)), ((), ()))
    qh = [q_ref[0, h] for h in heads]

    def later_key_matrix(n):
        r = lax.broadcasted_iota(jnp.int32, (n, n), 0)
        c = lax.broadcasted_iota(jnp.int32, (n, n), 1)
        return (c > r).astype(bf16)

    def scores(kblks, valid):
        zs = [lax.dot_general(kblks[h], qh[h], nt_dims, preferred_element_type=f32) for h in heads]
        if valid is not None:
            zs = [jnp.where(valid, z, NEG_INF) for z in zs]
        return zs

    def produce(buf, zs, lmat):
        nsub = lmat.shape[0]
        subs = range(zs[0].shape[0] // nsub)
        sps, laters = [], []
        for h in heads:
            sp = jnp.maximum(zs[h], 0.0) + jnp.log(1.0 + jnp.exp2(-jnp.abs(zs[h]))) * LOG2E
            buf[h] = zs[h] - sp
            sps.append([sp[i * nsub:i * nsub + 1, :] for i in subs])
            laters.append([jnp.dot(lmat, sp[i * nsub:(i + 1) * nsub].astype(bf16), preferred_element_type=f32)
                           for i in subs])
        totals = []
        for h in heads:
            after = None
            for i in reversed(subs):
                rows = slice(i * nsub, (i + 1) * nsub)
                later = laters[h][i] if after is None else laters[h][i] + after
                buf[h, rows] = buf[h, rows] - later
                sub_total = laters[h][i][0:1, :] + sps[h][i]
                after = sub_total if after is None else after + sub_total
            totals.append(after)
        return totals

    def consume(buf, vblks, run, acc):
        a = [jnp.exp2(buf[h] - run[h]).astype(bf16) for h in heads]
        return [acc[h] + lax.dot_general(vblks[h], a[h], tn_dims, preferred_element_type=f32) for h in heads]

    def k_block(j):
        start = pl.multiple_of(j * t, t)
        return [k_ref[0, h, pl.ds(start, t), :] for h in heads]

    def v_block(j):
        start = pl.multiple_of(j * t, t)
        return [v_ref[0, h, pl.ds(start, t), :] for h in heads]

    lmat_t = later_key_matrix(min(t, SB_CUMSUM_BLOCK))

    def step(cur, nxt, j, state):
        run_c, run_n, acc = state
        zs = scores(k_block(j - 1), None)
        acc = consume(cur, v_block(j), run_c, acc)
        totals = produce(nxt, zs, lmat_t)
        return (run_n, [run_n[h] + totals[h] for h in heads], acc)

    def last_steps(cur, state):
        run_c, run_n, acc = state
        zs = scores([km_ref[0, h] for h in heads], lax.broadcasted_iota(jnp.int32, (META_PAD, t), 0) < N_META)
        acc = consume(cur, v_block(0), run_c, acc)
        produce(d_meta, zs, later_key_matrix(META_PAD))
        return consume(d_meta, [vm_ref[0, h] for h in heads], run_n, acc)

    key = lax.broadcasted_iota(jnp.int32, (t, t), 0)
    qry = lax.broadcasted_iota(jnp.int32, (t, t), 1)
    totals = produce(d_a, scores(k_block(qi), key < qry), lmat_t)
    state = ([jnp.zeros((1, t), f32) for _ in heads], totals, [jnp.zeros((HEAD_W, t), f32) for _ in heads])

    def pair(n, st):
        st = step(d_a, d_b, qi - 2 * n, st)
        return step(d_b, d_a, qi - 2 * n - 1, st)

    def alive(st):
        lowest = st[0][0]
        for h in heads[1:]:
            lowest = jnp.minimum(lowest, st[0][h])
        return jnp.min(lowest) < SB_DEAD_LOG2

    def pair_while(carry):
        n, _, st = carry
        st = pair(n, st)
        return n + 1, alive(st), st

    _, still_alive, state = lax.while_loop(
        lambda carry: jnp.logical_and(carry[0] < qi // 2, carry[1]), pair_while, (jnp.int32(0), True, state))

    def odd_tail(st):
        return last_steps(d_b, step(d_a, d_b, 1, st))

    def even_tail(st):
        return last_steps(d_a, st)

    def remaining(st):
        return lax.cond(qi % 2 == 1, odd_tail, even_tail, st)

    acc_fin = lax.cond(still_alive, remaining, lambda st: st[2], state)
    for h in heads:
        acc = acc_fin[h]
        ms = jnp.mean(acc * acc, axis=0, keepdims=True)
        o = (acc * lax.rsqrt(ms + EPS)).T
        o_ref[:, h * HEAD_W:(h + 1) * HEAD_W] = (o * og_ref[...]).astype(bf16)


def _sb_attention(proj, proj_meta, og, *, t):
    return _attn_call(
        _sb_kernel, "sb_attn", [pl.BlockSpec((1, HEAD_W), lambda b, h, i: (0, 0))], (og,),
        proj, proj_meta, _attn_specs(proj.shape[2], t, SB_HEADS_PER_STEP, 3 * N_HEADS, 4 * N_HEADS, 5 * N_HEADS),
        [pltpu.VMEM((SB_HEADS_PER_STEP, META_PAD, t), f32),
         pltpu.VMEM((SB_HEADS_PER_STEP, t, t), f32),
         pltpu.VMEM((SB_HEADS_PER_STEP, t, t), f32)],
        t=t, hps=SB_HEADS_PER_STEP)


def _outproj_kernel(x_ref, md_ref, ms_ref, wd_ref, ws_ref, o_ref):
    o_ref[...] = (x_ref[...]
                  + jnp.dot(md_ref[...], wd_ref[...], preferred_element_type=f32)
                  + jnp.dot(ms_ref[...], ws_ref[...], preferred_element_type=f32))


def _outproj(x2d, mixed_diff, mixed_sb, w_out, *, tm):
    rows, d = x2d.shape
    half = mixed_diff.shape[1]
    return pl.pallas_call(
        _outproj_kernel,
        grid=(rows // tm,),
        in_specs=[
            pl.BlockSpec((tm, d), lambda i: (i, 0)),
            pl.BlockSpec((tm, half), lambda i: (i, 0)),
            pl.BlockSpec((tm, half), lambda i: (i, 0)),
            pl.BlockSpec((half, d), lambda i: (0, 0)),
            pl.BlockSpec((half, d), lambda i: (1, 0)),
        ],
        out_specs=pl.BlockSpec((tm, d), lambda i: (i, 0)),
        out_shape=jax.ShapeDtypeStruct((rows, d), f32),
        compiler_params=pltpu.CompilerParams(
            dimension_semantics=("parallel",), vmem_limit_bytes=VMEM_LIMIT),
        name="outproj",
    )(x2d, mixed_diff, mixed_sb, w_out, w_out)


def _mlp_kernel(h_ref, g_ref, wu_ref, wd_ref, o_ref, m_ref):
    @pl.when(pl.program_id(1) == 0)
    def _():
        h = h_ref[...]
        ms = jnp.mean(h * h, axis=-1, keepdims=True)
        m_ref[...] = (h * lax.rsqrt(ms + EPS) * g_ref[...]).astype(bf16)
        o_ref[...] = h

    hid = jnp.dot(m_ref[...], wu_ref[...], preferred_element_type=f32)
    hid = jnp.square(jnp.maximum(hid, 0.0))
    o_ref[...] += jnp.dot(hid.astype(bf16), wd_ref[...], preferred_element_type=f32)


def _mlp(h1, g_mlp, w_up, w_down, *, tm, tf):
    rows, d = h1.shape
    return pl.pallas_call(
        _mlp_kernel,
        grid=(rows // tm, w_up.shape[1] // tf),
        in_specs=[
            pl.BlockSpec((tm, d), lambda i, f: (i, 0)),
            pl.BlockSpec((1, d), lambda i, f: (0, 0)),
            pl.BlockSpec((d, tf), lambda i, f: (0, f)),
            pl.BlockSpec((tf, d), lambda i, f: (f, 0)),
        ],
        out_specs=pl.BlockSpec((tm, d), lambda i, f: (i, 0)),
        out_shape=jax.ShapeDtypeStruct((rows, d), f32),
        scratch_shapes=[pltpu.VMEM((tm, d), bf16)],
        compiler_params=pltpu.CompilerParams(
            dimension_semantics=("parallel", "arbitrary"), vmem_limit_bytes=VMEM_LIMIT),
        name="mlp",
    )(h1, g_mlp, w_up, w_down)


def _rope_tables(n_pos):
    pos = jnp.arange(n_pos, dtype=f32)
    inv = ROPE_THETA ** (-jnp.arange(0, QK_DIM, 2, dtype=f32) / QK_DIM)
    ang = pos[:, None] * inv[None, :]
    cos, sin = jnp.cos(ang), jnp.sin(ang)
    return jnp.concatenate([cos, cos, cos, cos], axis=-1), jnp.concatenate([-sin, sin, -sin, sin], axis=-1)


def kernel(x, meta_tokens, g_mix, w_in, q_norm_g, k_norm_g, lambda_q1, lambda_k1, lambda_q2, lambda_k2,
           diff_out_g, sb_out_g, w_out, g_mlp, w_up, w_down):
    batch, seq, d = x.shape
    assert g_mix.shape[0] == 1, "single-layer kernel"
    assert meta_tokens.shape[0] == N_META and seq % DIFF_BLOCK == 0 and seq % SB_BLOCK == 0

    x2d = x.reshape(batch * seq, d)
    w_in_b = w_in[0].astype(bf16)
    w_out_b = w_out[0].astype(bf16)
    w_up_b = w_up[0].astype(bf16)
    w_down_b = w_down[0].astype(bf16)
    qg = jnp.tile(q_norm_g[0], 2)[None, :]
    kg = jnp.tile(k_norm_g[0], 2)[None, :]
    cos_t, sin_t = _rope_tables(N_META + seq)

    proj = _inproj(x2d, g_mix, w_in_b, cos_t[N_META:], sin_t[N_META:], qg, kg,
                   batch=batch, seq=seq, tm=INPROJ_ROWS, tn=INPROJ_COLS)
    proj_meta = _inproj(meta_tokens.astype(f32), g_mix, w_in_b, cos_t[:N_META], sin_t[:N_META], qg, kg,
                        batch=1, seq=N_META, tm=N_META, tn=INPROJ_COLS)
    proj_meta = jnp.pad(proj_meta, ((0, 0), (0, 0), (0, META_PAD - N_META), (0, 0)))

    mixed_diff = _diff_attention(proj, proj_meta, lambda_q1, lambda_k1, lambda_q2, lambda_k2, diff_out_g,
                                 t=DIFF_BLOCK)
    mixed_sb = _sb_attention(proj, proj_meta, sb_out_g, t=SB_BLOCK)

    h1 = _outproj(x2d, mixed_diff, mixed_sb, w_out_b, tm=OUTPROJ_ROWS)
    out = _mlp(h1, g_mlp, w_up_b, w_down_b, tm=MLP_ROWS, tf=MLP_HIDDEN_COLS)
    return out.reshape(batch, seq, d)
```

```python
import functools
import math

import jax
import jax.numpy as jnp
from jax import lax
from jax.experimental import pallas as pl
from jax.experimental.pallas import tpu as pltpu

N_META = 16
N_HEADS = 8
HEAD_W = 128
QK_DIM = 64
N_GROUPS = 6 * N_HEADS
ROPE_THETA = 10000.0
EPS = 1e-6
NEG_INF = -1e30
LAMBDA_INIT = 0.8 - 0.6 * math.exp(0.0)
LOG2E = math.log2(math.e)

INPROJ_ROWS, INPROJ_COLS = 256, 512
OUTPROJ_ROWS = 512
MLP_ROWS, MLP_HIDDEN_COLS = 512, 1024
META_PAD = 128
DIFF_BLOCK = 512
SB_BLOCK = 256
SB_CUMSUM_BLOCK = 256
SB_DEAD_LOG2 = 160.0
DIFF_HEADS_PER_STEP = 4
SB_HEADS_PER_STEP = 4
VMEM_LIMIT = 56 * 1024 * 1024

f32 = jnp.float32
bf16 = jnp.bfloat16


def _lane_iota(shape):
    return lax.broadcasted_iota(jnp.int32, shape, len(shape) - 1)


def _inproj_kernel(x_ref, g_ref, w_ref, cos_ref, sin_ref, qg_ref, kg_ref, *refs, tn):
    n_cast = (len(refs) - 1) // 2
    o_ref = refs[n_cast]
    for src_ref, dst_ref in zip(refs[:n_cast], refs[n_cast + 1:]):
        dst_ref[...] = src_ref[...].astype(bf16)

    xf = x_ref[...]
    ms = jnp.mean(xf * xf, axis=-1, keepdims=True)
    u = (xf * lax.rsqrt(ms + EPS) * g_ref[...]).astype(bf16)

    ngrp = tn // HEAD_W
    lane = _lane_iota((1, HEAD_W))
    first = lane < QK_DIM
    low_half = (lane & (QK_DIM // 2)) == 0
    cos = cos_ref[...]
    sin = sin_ref[...]

    for j in range(w_ref.shape[1] // tn):
        acc = jnp.dot(u, w_ref[:, j * tn:(j + 1) * tn], preferred_element_type=f32)
        for g in range(ngrp):
            grp = j * ngrp + g
            t = acc[:, g * HEAD_W:(g + 1) * HEAD_W]
            if grp < 2 * N_HEADS:
                is_q = grp < N_HEADS
                ss = t * t
                tot = jnp.sum(ss, axis=-1, keepdims=True)
                lo = jnp.sum(jnp.where(first, ss, 0.0), axis=-1, keepdims=True)
                msq = jnp.where(first, lo, tot - lo) * (1.0 / QK_DIM)
                y = t * lax.rsqrt(msq + EPS) * (qg_ref[...] if is_q else kg_ref[...])
                rot = jnp.where(low_half, pltpu.roll(y, HEAD_W - QK_DIM // 2, 1), pltpu.roll(y, QK_DIM // 2, 1))
                t = y * cos + rot * sin
                if is_q:
                    t = t * (QK_DIM ** -0.5 * LOG2E)
            elif 3 * N_HEADS <= grp < 4 * N_HEADS:
                t = t * (HEAD_W ** -0.5 * LOG2E)
            o_ref[0, grp] = t.astype(bf16)


def _inproj(x2d, g_mix, w_in, cos_t, sin_t, qg, kg, *, batch, seq, tm, tn, cast=()):
    rows, d = x2d.shape
    nb = seq // tm
    steps = rows // tm

    def cast_specs():
        specs = []
        for arr, axis in cast:
            blk = tuple(n // steps if a == axis else n for a, n in enumerate(arr.shape))
            specs.append(pl.BlockSpec(blk, (lambda i: (i, 0)) if axis == 0 else (lambda i: (0, i))))
        return specs

    outs = pl.pallas_call(
        functools.partial(_inproj_kernel, tn=tn),
        grid=(rows // tm,),
        in_specs=[
            pl.BlockSpec((tm, d), lambda i: (i, 0)),
            pl.BlockSpec((1, d), lambda i: (0, 0)),
            pl.BlockSpec(w_in.shape, lambda i: (0, 0), pipeline_mode=pl.Buffered(1)),
            pl.BlockSpec((tm, HEAD_W), lambda i: (i % nb, 0)),
            pl.BlockSpec((tm, HEAD_W), lambda i: (i % nb, 0)),
            pl.BlockSpec((1, HEAD_W), lambda i: (0, 0)),
            pl.BlockSpec((1, HEAD_W), lambda i: (0, 0)),
        ] + cast_specs(),
        out_specs=[pl.BlockSpec((1, N_GROUPS, tm, HEAD_W), lambda i: (i // nb, 0, i % nb, 0))] + cast_specs(),
        out_shape=[jax.ShapeDtypeStruct((batch, N_GROUPS, seq, HEAD_W), bf16)]
        + [jax.ShapeDtypeStruct(arr.shape, bf16) for arr, _ in cast],
        compiler_params=pltpu.CompilerParams(dimension_semantics=("parallel",), vmem_limit_bytes=VMEM_LIMIT),
        name="inproj",
    )(x2d, g_mix, w_in, cos_t, sin_t, qg, kg, *[arr for arr, _ in cast])
    return outs if cast else outs[0]


def _attn_specs(seq, t, hps, q_grp, k_grp, v_grp):
    return [
        pl.BlockSpec((1, hps, t, HEAD_W), lambda b, h, i: (b, q_grp // hps + h, i, 0)),
        pl.BlockSpec((1, hps, seq, HEAD_W), lambda b, h, i: (b, k_grp // hps + h, 0, 0)),
        pl.BlockSpec((1, hps, seq, HEAD_W), lambda b, h, i: (b, v_grp // hps + h, 0, 0)),
        pl.BlockSpec((1, hps, META_PAD, HEAD_W), lambda b, h, i: (0, k_grp // hps + h, 0, 0)),
        pl.BlockSpec((1, hps, META_PAD, HEAD_W), lambda b, h, i: (0, v_grp // hps + h, 0, 0)),
    ]


def _attn_call(kernel_fn, name, small_specs, small_args, proj, proj_meta, grp_specs, scratch, *, t, hps):
    batch, _, seq, _ = proj.shape
    nq = seq // t
    return pl.pallas_call(
        functools.partial(kernel_fn, t=t, hps=hps),
        grid=(batch, N_HEADS // hps, nq),
        in_specs=small_specs + grp_specs,
        out_specs=pl.BlockSpec((t, hps * HEAD_W), lambda b, h, i: (b * nq + i, h)),
        out_shape=jax.ShapeDtypeStruct((batch * seq, N_HEADS * HEAD_W), bf16),
        scratch_shapes=scratch,
        compiler_params=pltpu.CompilerParams(
            dimension_semantics=("parallel", "parallel", "arbitrary"), vmem_limit_bytes=VMEM_LIMIT),
        name=name,
    )(*small_args, proj, proj, proj, proj_meta, proj_meta)


def _diff_kernel(lq1_ref, lk1_ref, lq2_ref, lk2_ref, og_ref, q_ref, k_ref, v_ref, km_ref, vm_ref, o_ref,
                 s_meta, s_a, s_b, *, t, hps):
    qi = pl.program_id(2)
    heads = range(hps)
    lane_q = _lane_iota((t, HEAD_W))
    nt_dims = (((1,), (1,)), ((), ()))
    tn_dims = (((0,), (0,)), ((), ()))

    def stacked_q(h):
        q = q_ref[0, h]
        zero = jnp.zeros_like(q)
        return jnp.concatenate([jnp.where(lane_q < QK_DIM, q, zero), jnp.where(lane_q >= QK_DIM, q, zero)], axis=0)

    qs = [stacked_q(h) for h in heads]

    lam = (jnp.exp(jnp.sum(lq1_ref[...] * lk1_ref[...], axis=-1, keepdims=True))
           - jnp.exp(jnp.sum(lq2_ref[...] * lk2_ref[...], axis=-1, keepdims=True))
           + LAMBDA_INIT)

    key = lax.broadcasted_iota(jnp.int32, (t, 2 * t), 0)
    qry = lax.broadcasted_iota(jnp.int32, (t, 2 * t), 1)
    causal = key <= jnp.where(qry >= t, qry - t, qry)

    def produce(buf, kblks, mask, m_in):
        m_out = []
        for h in heads:
            s = lax.dot_general(kblks[h], qs[h], nt_dims, preferred_element_type=f32)
            if mask is not None:
                s = jnp.where(mask, s, NEG_INF)
            buf[h] = s
            m_out.append(jnp.maximum(m_in[h], jnp.max(s, axis=0, keepdims=True)))
        return m_out

    def k_block(j):
        start = pl.multiple_of(j * t, t)
        return [k_ref[0, h, pl.ds(start, t), :] for h in heads]

    def consume(buf, vblks, state):
        m_prev, m_cur, l, acc = state
        alphas, ps, ls = [], [], []
        for h in heads:
            alpha = jnp.exp2(m_prev[h] - m_cur[h])
            p = jnp.exp2(buf[h] - m_cur[h])
            ls.append(alpha * l[h] + jnp.sum(p, axis=0, keepdims=True))
            alphas.append(alpha)
            ps.append(p.astype(bf16))
        accs = [alphas[h] * acc[h] + lax.dot_general(vblks[h], ps[h], tn_dims, preferred_element_type=f32)
                for h in heads]
        return ls, accs

    def v_block(j):
        start = pl.multiple_of(j * t, t)
        return [v_ref[0, h, pl.ds(start, t), :] for h in heads]

    def step(cur, nxt, j, state):
        m_next = produce(nxt, k_block(j - 1), None, state[1])
        l, acc = consume(cur, v_block(j), state)
        return (state[1], m_next, l, acc)

    def last_steps(cur, state):
        meta_mask = lax.broadcasted_iota(jnp.int32, (META_PAD, 2 * t), 0) < N_META
        m_meta = produce(s_meta, [km_ref[0, h] for h in heads], meta_mask, state[1])
        l, acc = consume(cur, v_block(0), state)
        return consume(s_meta, [vm_ref[0, h] for h in heads], (state[1], m_meta, l, acc))

    neg = [jnp.full((1, 2 * t), NEG_INF, f32) for _ in heads]
    m_diag = produce(s_a, k_block(qi), causal, neg)
    state = (neg, m_diag, [jnp.zeros((1, 2 * t), f32) for _ in heads],
             [jnp.zeros((HEAD_W, 2 * t), f32) for _ in heads])

    def pair(n, st):
        st = step(s_a, s_b, qi - 2 * n, st)
        return step(s_b, s_a, qi - 2 * n - 1, st)

    state = lax.fori_loop(0, qi // 2, pair, state)

    def odd_tail(st):
        return last_steps(s_b, step(s_a, s_b, 1, st))

    def even_tail(st):
        return last_steps(s_a, st)

    l_fin, acc_fin = lax.cond(qi % 2 == 1, odd_tail, even_tail, state)

    for h in heads:
        o = acc_fin[h] / l_fin[h]
        o = o[:, :t] - lam * o[:, t:]
        ms = jnp.mean(o * o, axis=0, keepdims=True)
        o = (o * lax.rsqrt(ms + EPS)).T
        o_ref[:, h * HEAD_W:(h + 1) * HEAD_W] = (o * og_ref[...] * (1.0 - LAMBDA_INIT)).astype(bf16)


def _diff_attention(proj, proj_meta, lq1, lk1, lq2, lk2, og, *, t):
    vec = lambda n: pl.BlockSpec((1, n), lambda b, h, i: (0, 0))
    return _attn_call(
        _diff_kernel, "diff_attn",
        [vec(QK_DIM), vec(QK_DIM), vec(QK_DIM), vec(QK_DIM), vec(HEAD_W)], (lq1, lk1, lq2, lk2, og),
        proj, proj_meta, _attn_specs(proj.shape[2], t, DIFF_HEADS_PER_STEP, 0, N_HEADS, 2 * N_HEADS),
        [pltpu.VMEM((DIFF_HEADS_PER_STEP, META_PAD, 2 * t), f32),
         pltpu.VMEM((DIFF_HEADS_PER_STEP, t, 2 * t), f32),
         pltpu.VMEM((DIFF_HEADS_PER_STEP, t, 2 * t), f32)],
        t=t, hps=DIFF_HEADS_PER_STEP)


def _sb_kernel(og_ref, q_ref, k_ref, v_ref, km_ref, vm_ref, o_ref, d_meta, d_a, d_b, *, t, hps):
    qi = pl.program_id(2)
    heads = range(hps)
    nt_dims = (((1,), (1,)), ((), ()))
    tn_dims = (((0,), (0,)), ((), ()))
    qh = [q_ref[0, h] for h in heads]

    def later_key_matrix(n):
        r = lax.broadcasted_iota(jnp.int32, (n, n), 0)
        c = lax.broadcasted_iota(jnp.int32, (n, n), 1)
        return (c > r).astype(bf16)

    def scores(kblks, valid):
        zs = [lax.dot_general(kblks[h], qh[h], nt_dims, preferred_element_type=f32) for h in heads]
        if valid is not None:
            zs = [jnp.where(valid, z, NEG_INF) for z in zs]
        return zs

    def produce(buf, zs, lmat):
        nsub = lmat.shape[0]
        subs = range(zs[0].shape[0] // nsub)
        sps, laters = [], []
        for h in heads:
            sp = jnp.maximum(zs[h], 0.0) + jnp.log(1.0 + jnp.exp2(-jnp.abs(zs[h]))) * LOG2E
            buf[h] = zs[h] - sp
            sps.append([sp[i * nsub:i * nsub + 1, :] for i in subs])
            laters.append([jnp.dot(lmat, sp[i * nsub:(i + 1) * nsub].astype(bf16), preferred_element_type=f32)
                           for i in subs])
        totals = []
        for h in heads:
            after = None
            for i in reversed(subs):
                rows = slice(i * nsub, (i + 1) * nsub)
                later = laters[h][i] if after is None else laters[h][i] + after
                buf[h, rows] = buf[h, rows] - later
                sub_total = laters[h][i][0:1, :] + sps[h][i]
                after = sub_total if after is None else after + sub_total
            totals.append(after)
        return totals

    def consume(buf, vblks, run, acc):
        a = [jnp.exp2(buf[h] - run[h]).astype(bf16) for h in heads]
        return [acc[h] + lax.dot_general(vblks[h], a[h], tn_dims, preferred_element_type=f32) for h in heads]

    def k_block(j):
        start = pl.multiple_of(j * t, t)
        return [k_ref[0, h, pl.ds(start, t), :] for h in heads]

    def v_block(j):
        start = pl.multiple_of(j * t, t)
        return [v_ref[0, h, pl.ds(start, t), :] for h in heads]

    lmat_t = later_key_matrix(min(t, SB_CUMSUM_BLOCK))

    def step(cur, nxt, j, state):
        run_c, run_n, acc = state
        zs = scores(k_block(j - 1), None)
        acc = consume(cur, v_block(j), run_c, acc)
        totals = produce(nxt, zs, lmat_t)
        return (run_n, [run_n[h] + totals[h] for h in heads], acc)

    def last_steps(cur, state):
        run_c, run_n, acc = state
        zs = scores([km_ref[0, h] for h in heads], lax.broadcasted_iota(jnp.int32, (META_PAD, t), 0) < N_META)
        acc = consume(cur, v_block(0), run_c, acc)
        produce(d_meta, zs, later_key_matrix(META_PAD))
        return consume(d_meta, [vm_ref[0, h] for h in heads], run_n, acc)

    key = lax.broadcasted_iota(jnp.int32, (t, t), 0)
    qry = lax.broadcasted_iota(jnp.int32, (t, t), 1)
    totals = produce(d_a, scores(k_block(qi), key < qry), lmat_t)
    state = ([jnp.zeros((1, t), f32) for _ in heads], totals, [jnp.zeros((HEAD_W, t), f32) for _ in heads])

    def pair(n, st):
        st = step(d_a, d_b, qi - 2 * n, st)
        return step(d_b, d_a, qi - 2 * n - 1, st)

    def alive(st):
        lowest = st[0][0]
        for h in heads[1:]:
            lowest = jnp.minimum(lowest, st[0][h])
        return jnp.min(lowest) < SB_DEAD_LOG2

    def pair_while(carry):
        n, _, st = carry
        st = pair(n, st)
        return n + 1, alive(st), st

    _, still_alive, state = lax.while_loop(
        lambda carry: jnp.logical_and(carry[0] < qi // 2, carry[1]), pair_while, (jnp.int32(0), True, state))

    def odd_tail(st):
        return last_steps(d_b, step(d_a, d_b, 1, st))

    def even_tail(st):
        return last_steps(d_a, st)

    def remaining(st):
        return lax.cond(qi % 2 == 1, odd_tail, even_tail, st)

    acc_fin = lax.cond(still_alive, remaining, lambda st: st[2], state)
    for h in heads:
        acc = acc_fin[h]
        ms = jnp.mean(acc * acc, axis=0, keepdims=True)
        o = (acc * lax.rsqrt(ms + EPS)).T
        o_ref[:, h * HEAD_W:(h + 1) * HEAD_W] = (o * og_ref[...]).astype(bf16)


def _sb_attention(proj, proj_meta, og, *, t):
    return _attn_call(
        _sb_kernel, "sb_attn", [pl.BlockSpec((1, HEAD_W), lambda b, h, i: (0, 0))], (og,),
        proj, proj_meta, _attn_specs(proj.shape[2], t, SB_HEADS_PER_STEP, 3 * N_HEADS, 4 * N_HEADS, 5 * N_HEADS),
        [pltpu.VMEM((SB_HEADS_PER_STEP, META_PAD, t), f32),
         pltpu.VMEM((SB_HEADS_PER_STEP, t, t), f32),
         pltpu.VMEM((SB_HEADS_PER_STEP, t, t), f32)],
        t=t, hps=SB_HEADS_PER_STEP)


def _outproj_kernel(x_ref, md_ref, ms_ref, wd_ref, ws_ref, g_ref, h_ref, m_ref):
    h = (x_ref[...]
         + jnp.dot(md_ref[...], wd_ref[...], preferred_element_type=f32)
         + jnp.dot(ms_ref[...], ws_ref[...], preferred_element_type=f32))
    h_ref[...] = h
    ms = jnp.mean(h * h, axis=-1, keepdims=True)
    m_ref[...] = (h * lax.rsqrt(ms + EPS) * g_ref[...]).astype(bf16)


def _outproj(x2d, mixed_diff, mixed_sb, w_out, g_mlp, *, tm):
    rows, d = x2d.shape
    half = mixed_diff.shape[1]
    return pl.pallas_call(
        _outproj_kernel,
        grid=(rows // tm,),
        in_specs=[
            pl.BlockSpec((tm, d), lambda i: (i, 0)),
            pl.BlockSpec((tm, half), lambda i: (i, 0)),
            pl.BlockSpec((tm, half), lambda i: (i, 0)),
            pl.BlockSpec((half, d), lambda i: (0, 0)),
            pl.BlockSpec((half, d), lambda i: (1, 0)),
            pl.BlockSpec((1, d), lambda i: (0, 0)),
        ],
        out_specs=[pl.BlockSpec((tm, d), lambda i: (i, 0)), pl.BlockSpec((tm, d), lambda i: (i, 0))],
        out_shape=[jax.ShapeDtypeStruct((rows, d), f32), jax.ShapeDtypeStruct((rows, d), bf16)],
        compiler_params=pltpu.CompilerParams(
            dimension_semantics=("parallel",), vmem_limit_bytes=VMEM_LIMIT),
        name="outproj",
    )(x2d, mixed_diff, mixed_sb, w_out, w_out, g_mlp)


def _mlp_kernel(h_ref, m_ref, wu_ref, wd_ref, o_ref):
    @pl.when(pl.program_id(1) == 0)
    def _():
        o_ref[...] = h_ref[...]

    hid = jnp.dot(m_ref[...], wu_ref[...], preferred_element_type=f32)
    hid = jnp.square(jnp.maximum(hid, 0.0))
    o_ref[...] += jnp.dot(hid.astype(bf16), wd_ref[...], preferred_element_type=f32)


def _mlp(h1, m, w_up, w_down, *, tm, tf):
    rows, d = h1.shape
    return pl.pallas_call(
        _mlp_kernel,
        grid=(rows // tm, w_up.shape[1] // tf),
        in_specs=[
            pl.BlockSpec((tm, d), lambda i, f: (i, 0)),
            pl.BlockSpec((tm, d), lambda i, f: (i, 0)),
            pl.BlockSpec((d, tf), lambda i, f: (0, f)),
            pl.BlockSpec((tf, d), lambda i, f: (f, 0)),
        ],
        out_specs=pl.BlockSpec((tm, d), lambda i, f: (i, 0)),
        out_shape=jax.ShapeDtypeStruct((rows, d), f32),
        compiler_params=pltpu.CompilerParams(
            dimension_semantics=("parallel", "arbitrary"), vmem_limit_bytes=VMEM_LIMIT),
        name="mlp",
    )(h1, m, w_up, w_down)


def _rope_tables(n_pos):
    pos = jnp.arange(n_pos, dtype=f32)
    inv = ROPE_THETA ** (-jnp.arange(0, QK_DIM, 2, dtype=f32) / QK_DIM)
    ang = pos[:, None] * inv[None, :]
    cos, sin = jnp.cos(ang), jnp.sin(ang)
    return jnp.concatenate([cos, cos, cos, cos], axis=-1), jnp.concatenate([-sin, sin, -sin, sin], axis=-1)


def kernel(x, meta_tokens, g_mix, w_in, q_norm_g, k_norm_g, lambda_q1, lambda_k1, lambda_q2, lambda_k2,
           diff_out_g, sb_out_g, w_out, g_mlp, w_up, w_down):
    batch, seq, d = x.shape
    assert g_mix.shape[0] == 1, "single-layer kernel"
    assert meta_tokens.shape[0] == N_META and seq % DIFF_BLOCK == 0 and seq % SB_BLOCK == 0

    x2d = x.reshape(batch * seq, d)
    w_in_b = w_in[0].astype(bf16)
    w_out_b = w_out[0].astype(bf16)
    qg = jnp.tile(q_norm_g[0], 2)[None, :]
    kg = jnp.tile(k_norm_g[0], 2)[None, :]
    cos_t, sin_t = _rope_tables(N_META + seq)

    proj, w_up_b, w_down_b = _inproj(x2d, g_mix, w_in_b, cos_t[N_META:], sin_t[N_META:], qg, kg,
                                     batch=batch, seq=seq, tm=INPROJ_ROWS, tn=INPROJ_COLS,
                                     cast=((w_up[0], 1), (w_down[0], 0)))
    proj_meta = _inproj(meta_tokens.astype(f32), g_mix, w_in_b, cos_t[:N_META], sin_t[:N_META], qg, kg,
                        batch=1, seq=N_META, tm=N_META, tn=INPROJ_COLS)
    proj_meta = jnp.pad(proj_meta, ((0, 0), (0, 0), (0, META_PAD - N_META), (0, 0)))

    mixed_diff = _diff_attention(proj, proj_meta, lambda_q1, lambda_k1, lambda_q2, lambda_k2, diff_out_g,
                                 t=DIFF_BLOCK)
    mixed_sb = _sb_attention(proj, proj_meta, sb_out_g, t=SB_BLOCK)

    h1, m = _outproj(x2d, mixed_diff, mixed_sb, w_out_b, g_mlp, tm=OUTPROJ_ROWS)
    out = _mlp(h1, m, w_up_b, w_down_b, tm=MLP_ROWS, tf=MLP_HIDDEN_COLS)
    return out.reshape(batch, seq, d)
```

```python
import functools
import math

import jax
import jax.numpy as jnp
from jax import lax
from jax.experimental import pallas as pl
from jax.experimental.pallas import tpu as pltpu

N_META = 16
N_HEADS = 8
HEAD_W = 128
QK_DIM = 64
N_GROUPS = 6 * N_HEADS
ROPE_THETA = 10000.0
EPS = 1e-6
NEG_INF = -1e30
LAMBDA_INIT = 0.8 - 0.6 * math.exp(0.0)
LOG2E = math.log2(math.e)

INPROJ_ROWS, INPROJ_COLS = 256, 512
OUTPROJ_ROWS = 512
MLP_ROWS, MLP_HIDDEN_COLS = 512, 1024
META_PAD = 128
DIFF_BLOCK = 512
SB_BLOCK = 256
SB_CUMSUM_BLOCK = 256
SB_DEAD_LOG2 = 160.0
DIFF_HEADS_PER_STEP = 4
SB_HEADS_PER_STEP = 8
VMEM_LIMIT = 56 * 1024 * 1024

f32 = jnp.float32
bf16 = jnp.bfloat16


def _lane_iota(shape):
    return lax.broadcasted_iota(jnp.int32, shape, len(shape) - 1)


def _inproj_kernel(x_ref, g_ref, w_ref, cos_ref, sin_ref, qg_ref, kg_ref, *refs, tn):
    n_cast = (len(refs) - 1) // 2
    o_ref = refs[n_cast]
    for src_ref, dst_ref in zip(refs[:n_cast], refs[n_cast + 1:]):
        dst_ref[...] = src_ref[...].astype(bf16)

    xf = x_ref[...]
    ms = jnp.mean(xf * xf, axis=-1, keepdims=True)
    u = (xf * lax.rsqrt(ms + EPS) * g_ref[...]).astype(bf16)

    ngrp = tn // HEAD_W
    lane = _lane_iota((1, HEAD_W))
    first = lane < QK_DIM
    low_half = (lane & (QK_DIM // 2)) == 0
    cos = cos_ref[...]
    sin = sin_ref[...]

    for j in range(w_ref.shape[1] // tn):
        acc = jnp.dot(u, w_ref[:, j * tn:(j + 1) * tn], preferred_element_type=f32)
        for g in range(ngrp):
            grp = j * ngrp + g
            t = acc[:, g * HEAD_W:(g + 1) * HEAD_W]
            if grp < 2 * N_HEADS:
                is_q = grp < N_HEADS
                ss = t * t
                tot = jnp.sum(ss, axis=-1, keepdims=True)
                lo = jnp.sum(jnp.where(first, ss, 0.0), axis=-1, keepdims=True)
                msq = jnp.where(first, lo, tot - lo) * (1.0 / QK_DIM)
                y = t * lax.rsqrt(msq + EPS) * (qg_ref[...] if is_q else kg_ref[...])
                rot = jnp.where(low_half, pltpu.roll(y, HEAD_W - QK_DIM // 2, 1), pltpu.roll(y, QK_DIM // 2, 1))
                t = y * cos + rot * sin
                if is_q:
                    t = t * (QK_DIM ** -0.5 * LOG2E)
            elif 3 * N_HEADS <= grp < 4 * N_HEADS:
                t = t * (HEAD_W ** -0.5 * LOG2E)
            o_ref[0, grp] = t.astype(bf16)


def _inproj(x2d, g_mix, w_in, cos_t, sin_t, qg, kg, *, batch, seq, tm, tn, cast=()):
    rows, d = x2d.shape
    nb = seq // tm
    steps = rows // tm

    def cast_specs():
        specs = []
        for arr, axis in cast:
            blk = tuple(n // steps if a == axis else n for a, n in enumerate(arr.shape))
            specs.append(pl.BlockSpec(blk, (lambda i: (i, 0)) if axis == 0 else (lambda i: (0, i))))
        return specs

    outs = pl.pallas_call(
        functools.partial(_inproj_kernel, tn=tn),
        grid=(rows // tm,),
        in_specs=[
            pl.BlockSpec((tm, d), lambda i: (i, 0)),
            pl.BlockSpec((1, d), lambda i: (0, 0)),
            pl.BlockSpec(w_in.shape, lambda i: (0, 0), pipeline_mode=pl.Buffered(1)),
            pl.BlockSpec((tm, HEAD_W), lambda i: (i % nb, 0)),
            pl.BlockSpec((tm, HEAD_W), lambda i: (i % nb, 0)),
            pl.BlockSpec((1, HEAD_W), lambda i: (0, 0)),
            pl.BlockSpec((1, HEAD_W), lambda i: (0, 0)),
        ] + cast_specs(),
        out_specs=[pl.BlockSpec((1, N_GROUPS, tm, HEAD_W), lambda i: (i // nb, 0, i % nb, 0))] + cast_specs(),
        out_shape=[jax.ShapeDtypeStruct((batch, N_GROUPS, seq, HEAD_W), bf16)]
        + [jax.ShapeDtypeStruct(arr.shape, bf16) for arr, _ in cast],
        compiler_params=pltpu.CompilerParams(dimension_semantics=("parallel",), vmem_limit_bytes=VMEM_LIMIT),
        name="inproj",
    )(x2d, g_mix, w_in, cos_t, sin_t, qg, kg, *[arr for arr, _ in cast])
    return outs if cast else outs[0]


def _attn_specs(seq, t, hps, q_grp, k_grp, v_grp):
    return [
        pl.BlockSpec((1, hps, t, HEAD_W), lambda b, h, i: (b, q_grp // hps + h, i, 0)),
        pl.BlockSpec((1, hps, seq, HEAD_W), lambda b, h, i: (b, k_grp // hps + h, 0, 0)),
        pl.BlockSpec((1, hps, seq, HEAD_W), lambda b, h, i: (b, v_grp // hps + h, 0, 0)),
        pl.BlockSpec((1, hps, META_PAD, HEAD_W), lambda b, h, i: (0, k_grp // hps + h, 0, 0)),
        pl.BlockSpec((1, hps, META_PAD, HEAD_W), lambda b, h, i: (0, v_grp // hps + h, 0, 0)),
    ]


def _attn_call(kernel_fn, name, small_specs, small_args, proj, proj_meta, grp_specs, scratch, *, t, hps):
    batch, _, seq, _ = proj.shape
    nq = seq // t
    return pl.pallas_call(
        functools.partial(kernel_fn, t=t, hps=hps),
        grid=(batch, N_HEADS // hps, nq),
        in_specs=small_specs + grp_specs,
        out_specs=pl.BlockSpec((t, hps * HEAD_W), lambda b, h, i: (b * nq + i, h)),
        out_shape=jax.ShapeDtypeStruct((batch * seq, N_HEADS * HEAD_W), bf16),
        scratch_shapes=scratch,
        compiler_params=pltpu.CompilerParams(
            dimension_semantics=("parallel", "parallel", "arbitrary"), vmem_limit_bytes=VMEM_LIMIT),
        name=name,
    )(*small_args, proj, proj, proj, proj_meta, proj_meta)


def _diff_kernel(lq1_ref, lk1_ref, lq2_ref, lk2_ref, og_ref, q_ref, k_ref, v_ref, km_ref, vm_ref, o_ref,
                 s_meta, s_a, s_b, *, t, hps):
    qi = pl.program_id(2)
    heads = range(hps)
    lane_q = _lane_iota((t, HEAD_W))
    nt_dims = (((1,), (1,)), ((), ()))
    tn_dims = (((0,), (0,)), ((), ()))

    def stacked_q(h):
        q = q_ref[0, h]
        zero = jnp.zeros_like(q)
        return jnp.concatenate([jnp.where(lane_q < QK_DIM, q, zero), jnp.where(lane_q >= QK_DIM, q, zero)], axis=0)

    qs = [stacked_q(h) for h in heads]

    lam = (jnp.exp(jnp.sum(lq1_ref[...] * lk1_ref[...], axis=-1, keepdims=True))
           - jnp.exp(jnp.sum(lq2_ref[...] * lk2_ref[...], axis=-1, keepdims=True))
           + LAMBDA_INIT)

    key = lax.broadcasted_iota(jnp.int32, (t, 2 * t), 0)
    qry = lax.broadcasted_iota(jnp.int32, (t, 2 * t), 1)
    causal = key <= jnp.where(qry >= t, qry - t, qry)

    def produce(buf, kblks, mask, m_in):
        m_out = []
        for h in heads:
            s = lax.dot_general(kblks[h], qs[h], nt_dims, preferred_element_type=f32)
            if mask is not None:
                s = jnp.where(mask, s, NEG_INF)
            buf[h] = s
            m_out.append(jnp.maximum(m_in[h], jnp.max(s, axis=0, keepdims=True)))
        return m_out

    def k_block(j):
        start = pl.multiple_of(j * t, t)
        return [k_ref[0, h, pl.ds(start, t), :] for h in heads]

    def consume(buf, vblks, state):
        m_prev, m_cur, l, acc = state
        alphas, ps, ls = [], [], []
        for h in heads:
            alpha = jnp.exp2(m_prev[h] - m_cur[h])
            p = jnp.exp2(buf[h] - m_cur[h])
            ls.append(alpha * l[h] + jnp.sum(p, axis=0, keepdims=True))
            alphas.append(alpha)
            ps.append(p.astype(bf16))
        accs = [alphas[h] * acc[h] + lax.dot_general(vblks[h], ps[h], tn_dims, preferred_element_type=f32)
                for h in heads]
        return ls, accs

    def v_block(j):
        start = pl.multiple_of(j * t, t)
        return [v_ref[0, h, pl.ds(start, t), :] for h in heads]

    def step(cur, nxt, j, state):
        m_next = produce(nxt, k_block(j - 1), None, state[1])
        l, acc = consume(cur, v_block(j), state)
        return (state[1], m_next, l, acc)

    def last_steps(cur, state):
        meta_mask = lax.broadcasted_iota(jnp.int32, (META_PAD, 2 * t), 0) < N_META
        m_meta = produce(s_meta, [km_ref[0, h] for h in heads], meta_mask, state[1])
        l, acc = consume(cur, v_block(0), state)
        return consume(s_meta, [vm_ref[0, h] for h in heads], (state[1], m_meta, l, acc))

    neg = [jnp.full((1, 2 * t), NEG_INF, f32) for _ in heads]
    m_diag = produce(s_a, k_block(qi), causal, neg)
    state = (neg, m_diag, [jnp.zeros((1, 2 * t), f32) for _ in heads],
             [jnp.zeros((HEAD_W, 2 * t), f32) for _ in heads])

    def pair(n, st):
        st = step(s_a, s_b, qi - 2 * n, st)
        return step(s_b, s_a, qi - 2 * n - 1, st)

    state = lax.fori_loop(0, qi // 2, pair, state)

    def odd_tail(st):
        return last_steps(s_b, step(s_a, s_b, 1, st))

    def even_tail(st):
        return last_steps(s_a, st)

    l_fin, acc_fin = lax.cond(qi % 2 == 1, odd_tail, even_tail, state)

    for h in heads:
        o = acc_fin[h] / l_fin[h]
        o = o[:, :t] - lam * o[:, t:]
        ms = jnp.mean(o * o, axis=0, keepdims=True)
        o = (o * lax.rsqrt(ms + EPS)).T
        o_ref[:, h * HEAD_W:(h + 1) * HEAD_W] = (o * og_ref[...] * (1.0 - LAMBDA_INIT)).astype(bf16)


def _diff_attention(proj, proj_meta, lq1, lk1, lq2, lk2, og, *, t):
    vec = lambda n: pl.BlockSpec((1, n), lambda b, h, i: (0, 0))
    return _attn_call(
        _diff_kernel, "diff_attn",
        [vec(QK_DIM), vec(QK_DIM), vec(QK_DIM), vec(QK_DIM), vec(HEAD_W)], (lq1, lk1, lq2, lk2, og),
        proj, proj_meta, _attn_specs(proj.shape[2], t, DIFF_HEADS_PER_STEP, 0, N_HEADS, 2 * N_HEADS),
        [pltpu.VMEM((DIFF_HEADS_PER_STEP, META_PAD, 2 * t), f32),
         pltpu.VMEM((DIFF_HEADS_PER_STEP, t, 2 * t), f32),
         pltpu.VMEM((DIFF_HEADS_PER_STEP, t, 2 * t), f32)],
        t=t, hps=DIFF_HEADS_PER_STEP)


def _sb_kernel(og_ref, q_ref, k_ref, v_ref, km_ref, vm_ref, o_ref, d_meta, d_a, d_b, *, t, hps):
    qi = pl.program_id(2)
    heads = range(hps)
    nt_dims = (((1,), (1,)), ((), ()))
    tn_dims = (((0,), (0,)), ((), ()))
    qh = [q_ref[0, h] for h in heads]

    def later_key_matrix(n):
        r = lax.broadcasted_iota(jnp.int32, (n, n), 0)
        c = lax.broadcasted_iota(jnp.int32, (n, n), 1)
        return (c > r).astype(bf16)

    def scores(kblks, valid):
        zs = [lax.dot_general(kblks[h], qh[h], nt_dims, preferred_element_type=f32) for h in heads]
        if valid is not None:
            zs = [jnp.where(valid, z, NEG_INF) for z in zs]
        return zs

    def produce(buf, zs, lmat):
        nsub = lmat.shape[0]
        subs = range(zs[0].shape[0] // nsub)
        sps, laters = [], []
        for h in heads:
            sp = jnp.maximum(zs[h], 0.0) + jnp.log(1.0 + jnp.exp2(-jnp.abs(zs[h]))) * LOG2E
            buf[h] = zs[h] - sp
            sps.append([sp[i * nsub:i * nsub + 1, :] for i in subs])
            laters.append([jnp.dot(lmat, sp[i * nsub:(i + 1) * nsub].astype(bf16), preferred_element_type=f32)
                           for i in subs])
        totals = []
        for h in heads:
            after = None
            for i in reversed(subs):
                rows = slice(i * nsub, (i + 1) * nsub)
                later = laters[h][i] if after is None else laters[h][i] + after
                buf[h, rows] = buf[h, rows] - later
                sub_total = laters[h][i][0:1, :] + sps[h][i]
                after = sub_total if after is None else after + sub_total
            totals.append(after)
        return totals

    def consume(buf, vblks, run, acc):
        a = [jnp.exp2(buf[h] - run[h]).astype(bf16) for h in heads]
        return [acc[h] + lax.dot_general(vblks[h], a[h], tn_dims, preferred_element_type=f32) for h in heads]

    def k_block(j):
        start = pl.multiple_of(j * t, t)
        return [k_ref[0, h, pl.ds(start, t), :] for h in heads]

    def v_block(j):
        start = pl.multiple_of(j * t, t)
        return [v_ref[0, h, pl.ds(start, t), :] for h in heads]

    lmat_t = later_key_matrix(min(t, SB_CUMSUM_BLOCK))

    def step(cur, nxt, j, state):
        run_c, run_n, acc = state
        zs = scores(k_block(j - 1), None)
        acc = consume(cur, v_block(j), run_c, acc)
        totals = produce(nxt, zs, lmat_t)
        return (run_n, [run_n[h] + totals[h] for h in heads], acc)

    def last_steps(cur, state):
        run_c, run_n, acc = state
        zs = scores([km_ref[0, h] for h in heads], lax.broadcasted_iota(jnp.int32, (META_PAD, t), 0) < N_META)
        acc = consume(cur, v_block(0), run_c, acc)
        produce(d_meta, zs, later_key_matrix(META_PAD))
        return consume(d_meta, [vm_ref[0, h] for h in heads], run_n, acc)

    key = lax.broadcasted_iota(jnp.int32, (t, t), 0)
    qry = lax.broadcasted_iota(jnp.int32, (t, t), 1)
    totals = produce(d_a, scores(k_block(qi), key < qry), lmat_t)
    state = ([jnp.zeros((1, t), f32) for _ in heads], totals, [jnp.zeros((HEAD_W, t), f32) for _ in heads])

    def pair(n, st):
        st = step(d_a, d_b, qi - 2 * n, st)
        return step(d_b, d_a, qi - 2 * n - 1, st)

    def alive(st):
        lowest = st[0][0]
        for h in heads[1:]:
            lowest = jnp.minimum(lowest, st[0][h])
        return jnp.min(lowest) < SB_DEAD_LOG2

    def pair_while(carry):
        n, _, st = carry
        st = pair(n, st)
        return n + 1, alive(st), st

    _, still_alive, state = lax.while_loop(
        lambda carry: jnp.logical_and(carry[0] < qi // 2, carry[1]), pair_while, (jnp.int32(0), True, state))

    def odd_tail(st):
        return last_steps(d_b, step(d_a, d_b, 1, st))

    def even_tail(st):
        return last_steps(d_a, st)

    def remaining(st):
        return lax.cond(qi % 2 == 1, odd_tail, even_tail, st)

    acc_fin = lax.cond(still_alive, remaining, lambda st: st[2], state)
    for h in heads:
        acc = acc_fin[h]
        ms = jnp.mean(acc * acc, axis=0, keepdims=True)
        o = (acc * lax.rsqrt(ms + EPS)).T
        o_ref[:, h * HEAD_W:(h + 1) * HEAD_W] = (o * og_ref[...]).astype(bf16)


def _sb_attention(proj, proj_meta, og, *, t):
    return _attn_call(
        _sb_kernel, "sb_attn", [pl.BlockSpec((1, HEAD_W), lambda b, h, i: (0, 0))], (og,),
        proj, proj_meta, _attn_specs(proj.shape[2], t, SB_HEADS_PER_STEP, 3 * N_HEADS, 4 * N_HEADS, 5 * N_HEADS),
        [pltpu.VMEM((SB_HEADS_PER_STEP, META_PAD, t), f32),
         pltpu.VMEM((SB_HEADS_PER_STEP, t, t), f32),
         pltpu.VMEM((SB_HEADS_PER_STEP, t, t), f32)],
        t=t, hps=SB_HEADS_PER_STEP)


def _outproj_kernel(x_ref, md_ref, ms_ref, wd_ref, ws_ref, o_ref):
    o_ref[...] = (x_ref[...]
                  + jnp.dot(md_ref[...], wd_ref[...], preferred_element_type=f32)
                  + jnp.dot(ms_ref[...], ws_ref[...], preferred_element_type=f32))


def _outproj(x2d, mixed_diff, mixed_sb, w_out, *, tm):
    rows, d = x2d.shape
    half = mixed_diff.shape[1]
    return pl.pallas_call(
        _outproj_kernel,
        grid=(rows // tm,),
        in_specs=[
            pl.BlockSpec((tm, d), lambda i: (i, 0)),
            pl.BlockSpec((tm, half), lambda i: (i, 0)),
            pl.BlockSpec((tm, half), lambda i: (i, 0)),
            pl.BlockSpec((half, d), lambda i: (0, 0)),
            pl.BlockSpec((half, d), lambda i: (1, 0)),
        ],
        out_specs=pl.BlockSpec((tm, d), lambda i: (i, 0)),
        out_shape=jax.ShapeDtypeStruct((rows, d), f32),
        compiler_params=pltpu.CompilerParams(
            dimension_semantics=("parallel",), vmem_limit_bytes=VMEM_LIMIT),
        name="outproj",
    )(x2d, mixed_diff, mixed_sb, w_out, w_out)


def _mlp_kernel(h_ref, g_ref, wu_ref, wd_ref, o_ref, m_ref):
    @pl.when(pl.program_id(1) == 0)
    def _():
        h = h_ref[...]
        ms = jnp.mean(h * h, axis=-1, keepdims=True)
        m_ref[...] = (h * lax.rsqrt(ms + EPS) * g_ref[...]).astype(bf16)
        o_ref[...] = h

    hid = jnp.dot(m_ref[...], wu_ref[...], preferred_element_type=f32)
    hid = jnp.square(jnp.maximum(hid, 0.0))
    o_ref[...] += jnp.dot(hid.astype(bf16), wd_ref[...], preferred_element_type=f32)


def _mlp(h1, g_mlp, w_up, w_down, *, tm, tf):
    rows, d = h1.shape
    return pl.pallas_call(
        _mlp_kernel,
        grid=(rows // tm, w_up.shape[1] // tf),
        in_specs=[
            pl.BlockSpec((tm, d), lambda i, f: (i, 0)),
            pl.BlockSpec((1, d), lambda i, f: (0, 0)),
            pl.BlockSpec((d, tf), lambda i, f: (0, f)),
            pl.BlockSpec((tf, d), lambda i, f: (f, 0)),
        ],
        out_specs=pl.BlockSpec((tm, d), lambda i, f: (i, 0)),
        out_shape=jax.ShapeDtypeStruct((rows, d), f32),
        scratch_shapes=[pltpu.VMEM((tm, d), bf16)],
        compiler_params=pltpu.CompilerParams(
            dimension_semantics=("parallel", "arbitrary"), vmem_limit_bytes=VMEM_LIMIT),
        name="mlp",
    )(h1, g_mlp, w_up, w_down)


def _rope_tables(n_pos):
    pos = jnp.arange(n_pos, dtype=f32)
    inv = ROPE_THETA ** (-jnp.arange(0, QK_DIM, 2, dtype=f32) / QK_DIM)
    ang = pos[:, None] * inv[None, :]
    cos, sin = jnp.cos(ang), jnp.sin(ang)
    return jnp.concatenate([cos, cos, cos, cos], axis=-1), jnp.concatenate([-sin, sin, -sin, sin], axis=-1)


def kernel(x, meta_tokens, g_mix, w_in, q_norm_g, k_norm_g, lambda_q1, lambda_k1, lambda_q2, lambda_k2,
           diff_out_g, sb_out_g, w_out, g_mlp, w_up, w_down):
    batch, seq, d = x.shape
    assert g_mix.shape[0] == 1, "single-layer kernel"
    assert meta_tokens.shape[0] == N_META and seq % DIFF_BLOCK == 0 and seq % SB_BLOCK == 0

    x2d = x.reshape(batch * seq, d)
    w_in_b = w_in[0].astype(bf16)
    qg = jnp.tile(q_norm_g[0], 2)[None, :]
    kg = jnp.tile(k_norm_g[0], 2)[None, :]
    cos_t, sin_t = _rope_tables(N_META + seq)

    proj, w_up_b, w_down_b, w_out_b = _inproj(x2d, g_mix, w_in_b, cos_t[N_META:], sin_t[N_META:], qg, kg,
                                              batch=batch, seq=seq, tm=INPROJ_ROWS, tn=INPROJ_COLS,
                                              cast=((w_up[0], 1), (w_down[0], 0), (w_out[0], 0)))
    proj_meta = _inproj(meta_tokens.astype(f32), g_mix, w_in_b, cos_t[:N_META], sin_t[:N_META], qg, kg,
                        batch=1, seq=N_META, tm=N_META, tn=INPROJ_COLS)
    proj_meta = jnp.pad(proj_meta, ((0, 0), (0, 0), (0, META_PAD - N_META), (0, 0)))

    mixed_diff = _diff_attention(proj, proj_meta, lambda_q1, lambda_k1, lambda_q2, lambda_k2, diff_out_g,
                                 t=DIFF_BLOCK)
    mixed_sb = _sb_attention(proj, proj_meta, sb_out_g, t=SB_BLOCK)

    h1 = _outproj(x2d, mixed_diff, mixed_sb, w_out_b, tm=OUTPROJ_ROWS)
    out = _mlp(h1, g_mlp, w_up_b, w_down_b, tm=MLP_ROWS, tf=MLP_HIDDEN_COLS)
    return out.reshape(batch, seq, d)
```

```python
import functools
import math

import jax
import jax.numpy as jnp
import numpy as np
from jax import lax
from jax.experimental import pallas as pl
from jax.experimental.pallas import tpu as pltpu

N_META = 16
N_HEADS = 8
HEAD_W = 128
QK_DIM = 64
N_GROUPS = 6 * N_HEADS
ROPE_THETA = 10000.0
EPS = 1e-6
NEG_INF = -1e30
LAMBDA_INIT = 0.8 - 0.6 * math.exp(0.0)
LOG2E = math.log2(math.e)

INPROJ_ROWS, INPROJ_COLS = 256, 512
OUTPROJ_ROWS = 512
MLP_ROWS, MLP_HIDDEN_COLS = 512, 1024
META_PAD = 128
DIFF_BLOCK = 512
SB_BLOCK = 256
SB_CUMSUM_BLOCK = 256
SB_DEAD_LOG2 = 160.0
DIFF_HEADS_PER_STEP = 4
SB_HEADS_PER_STEP = 8
VMEM_LIMIT = 56 * 1024 * 1024

f32 = jnp.float32
bf16 = jnp.bfloat16


def _lane_iota(shape):
    return lax.broadcasted_iota(jnp.int32, shape, len(shape) - 1)


def _inproj_kernel(x_ref, g_ref, w_ref, cos_ref, sin_ref, qg_ref, kg_ref, *refs, tn):
    n_cast = (len(refs) - 1) // 2
    o_ref = refs[n_cast]
    for src_ref, dst_ref in zip(refs[:n_cast], refs[n_cast + 1:]):
        dst_ref[...] = src_ref[...].astype(bf16)

    xf = x_ref[...]
    ms = jnp.mean(xf * xf, axis=-1, keepdims=True)
    u = (xf * lax.rsqrt(ms + EPS) * g_ref[...]).astype(bf16)

    ngrp = tn // HEAD_W
    lane = _lane_iota((1, HEAD_W))
    first = lane < QK_DIM
    low_half = (lane & (QK_DIM // 2)) == 0
    cos = cos_ref[...]
    sin = sin_ref[...]

    for j in range(w_ref.shape[1] // tn):
        acc = jnp.dot(u, w_ref[:, j * tn:(j + 1) * tn], preferred_element_type=f32)
        for g in range(ngrp):
            grp = j * ngrp + g
            t = acc[:, g * HEAD_W:(g + 1) * HEAD_W]
            if grp < 2 * N_HEADS:
                is_q = grp < N_HEADS
                ss = t * t
                tot = jnp.sum(ss, axis=-1, keepdims=True)
                lo = jnp.sum(jnp.where(first, ss, 0.0), axis=-1, keepdims=True)
                msq = jnp.where(first, lo, tot - lo) * (1.0 / QK_DIM)
                y = t * lax.rsqrt(msq + EPS) * (qg_ref[...] if is_q else kg_ref[...])
                rot = jnp.where(low_half, pltpu.roll(y, HEAD_W - QK_DIM // 2, 1), pltpu.roll(y, QK_DIM // 2, 1))
                t = y * cos + rot * sin
                if is_q:
                    t = t * (QK_DIM ** -0.5 * LOG2E)
            elif 3 * N_HEADS <= grp < 4 * N_HEADS:
                t = t * (HEAD_W ** -0.5 * LOG2E)
            o_ref[0, grp] = t.astype(bf16)


def _inproj(x2d, g_mix, w_in, cos_t, sin_t, qg, kg, *, batch, seq, tm, tn, cast=()):
    rows, d = x2d.shape
    nb = seq // tm
    steps = rows // tm

    def cast_specs():
        specs = []
        for arr, axis in cast:
            blk = tuple(n // steps if a == axis else n for a, n in enumerate(arr.shape))
            specs.append(pl.BlockSpec(blk, (lambda i: (i, 0)) if axis == 0 else (lambda i: (0, i))))
        return specs

    outs = pl.pallas_call(
        functools.partial(_inproj_kernel, tn=tn),
        grid=(rows // tm,),
        in_specs=[
            pl.BlockSpec((tm, d), lambda i: (i, 0)),
            pl.BlockSpec((1, d), lambda i: (0, 0)),
            pl.BlockSpec(w_in.shape, lambda i: (0, 0), pipeline_mode=pl.Buffered(1)),
            pl.BlockSpec((tm, HEAD_W), lambda i: (i % nb, 0)),
            pl.BlockSpec((tm, HEAD_W), lambda i: (i % nb, 0)),
            pl.BlockSpec((1, HEAD_W), lambda i: (0, 0)),
            pl.BlockSpec((1, HEAD_W), lambda i: (0, 0)),
        ] + cast_specs(),
        out_specs=[pl.BlockSpec((1, N_GROUPS, tm, HEAD_W), lambda i: (i // nb, 0, i % nb, 0))] + cast_specs(),
        out_shape=[jax.ShapeDtypeStruct((batch, N_GROUPS, seq, HEAD_W), bf16)]
        + [jax.ShapeDtypeStruct(arr.shape, bf16) for arr, _ in cast],
        compiler_params=pltpu.CompilerParams(dimension_semantics=("parallel",), vmem_limit_bytes=VMEM_LIMIT),
        name="inproj",
    )(x2d, g_mix, w_in, cos_t, sin_t, qg, kg, *[arr for arr, _ in cast])
    return outs if cast else outs[0]


def _attn_specs(seq, t, hps, q_grp, k_grp, v_grp):
    return [
        pl.BlockSpec((1, hps, t, HEAD_W), lambda b, h, i: (b, q_grp // hps + h, i, 0)),
        pl.BlockSpec((1, hps, seq, HEAD_W), lambda b, h, i: (b, k_grp // hps + h, 0, 0)),
        pl.BlockSpec((1, hps, seq, HEAD_W), lambda b, h, i: (b, v_grp // hps + h, 0, 0)),
        pl.BlockSpec((1, hps, META_PAD, HEAD_W), lambda b, h, i: (0, k_grp // hps + h, 0, 0)),
        pl.BlockSpec((1, hps, META_PAD, HEAD_W), lambda b, h, i: (0, v_grp // hps + h, 0, 0)),
    ]


def _attn_call(kernel_fn, name, small_specs, small_args, proj, proj_meta, grp_specs, scratch, *, t, hps):
    batch, _, seq, _ = proj.shape
    nq = seq // t
    return pl.pallas_call(
        functools.partial(kernel_fn, t=t, hps=hps),
        grid=(batch, N_HEADS // hps, nq),
        in_specs=small_specs + grp_specs,
        out_specs=pl.BlockSpec((t, hps * HEAD_W), lambda b, h, i: (b * nq + i, h)),
        out_shape=jax.ShapeDtypeStruct((batch * seq, N_HEADS * HEAD_W), bf16),
        scratch_shapes=scratch,
        compiler_params=pltpu.CompilerParams(
            dimension_semantics=("parallel", "parallel", "arbitrary"), vmem_limit_bytes=VMEM_LIMIT),
        name=name,
    )(*small_args, proj, proj, proj, proj_meta, proj_meta)


def _diff_kernel(lq1_ref, lk1_ref, lq2_ref, lk2_ref, og_ref, q_ref, k_ref, v_ref, km_ref, vm_ref, o_ref,
                 s_meta, s_a, s_b, *, t, hps):
    qi = pl.program_id(2)
    heads = range(hps)
    lane_q = _lane_iota((t, HEAD_W))
    nt_dims = (((1,), (1,)), ((), ()))
    tn_dims = (((0,), (0,)), ((), ()))

    def stacked_q(h):
        q = q_ref[0, h]
        zero = jnp.zeros_like(q)
        return jnp.concatenate([jnp.where(lane_q < QK_DIM, q, zero), jnp.where(lane_q >= QK_DIM, q, zero)], axis=0)

    qs = [stacked_q(h) for h in heads]

    lam = (jnp.exp(jnp.sum(lq1_ref[...] * lk1_ref[...], axis=-1, keepdims=True))
           - jnp.exp(jnp.sum(lq2_ref[...] * lk2_ref[...], axis=-1, keepdims=True))
           + LAMBDA_INIT)

    key = lax.broadcasted_iota(jnp.int32, (t, 2 * t), 0)
    qry = lax.broadcasted_iota(jnp.int32, (t, 2 * t), 1)
    causal = key <= jnp.where(qry >= t, qry - t, qry)

    def produce(buf, kblks, mask, m_in):
        m_out = []
        for h in heads:
            s = lax.dot_general(kblks[h], qs[h], nt_dims, preferred_element_type=f32)
            if mask is not None:
                s = jnp.where(mask, s, NEG_INF)
            buf[h] = s
            m_out.append(jnp.maximum(m_in[h], jnp.max(s, axis=0, keepdims=True)))
        return m_out

    def k_block(j):
        start = pl.multiple_of(j * t, t)
        return [k_ref[0, h, pl.ds(start, t), :] for h in heads]

    def consume(buf, vblks, state):
        m_prev, m_cur, l, acc = state
        ls, accs = [], []
        for h in heads:
            alpha = jnp.exp2(m_prev[h] - m_cur[h])
            p = jnp.exp2(buf[h] - m_cur[h])
            ls.append(alpha * l[h] + jnp.sum(p, axis=0, keepdims=True))
            accs.append(alpha * acc[h]
                        + lax.dot_general(vblks[h], p.astype(bf16), tn_dims, preferred_element_type=f32))
        return ls, accs

    def v_block(j):
        start = pl.multiple_of(j * t, t)
        return [v_ref[0, h, pl.ds(start, t), :] for h in heads]

    def step(cur, nxt, j, state):
        m_next = produce(nxt, k_block(j - 1), None, state[1])
        l, acc = consume(cur, v_block(j), state)
        return (state[1], m_next, l, acc)

    def last_steps(cur, state):
        meta_mask = lax.broadcasted_iota(jnp.int32, (META_PAD, 2 * t), 0) < N_META
        m_meta = produce(s_meta, [km_ref[0, h] for h in heads], meta_mask, state[1])
        l, acc = consume(cur, v_block(0), state)
        return consume(s_meta, [vm_ref[0, h] for h in heads], (state[1], m_meta, l, acc))

    neg = [jnp.full((1, 2 * t), NEG_INF, f32) for _ in heads]
    m_diag = produce(s_a, k_block(qi), causal, neg)
    state = (neg, m_diag, [jnp.zeros((1, 2 * t), f32) for _ in heads],
             [jnp.zeros((HEAD_W, 2 * t), f32) for _ in heads])

    def pair(n, st):
        st = step(s_a, s_b, qi - 2 * n, st)
        return step(s_b, s_a, qi - 2 * n - 1, st)

    state = lax.fori_loop(0, qi // 2, pair, state)

    def odd_tail(st):
        return last_steps(s_b, step(s_a, s_b, 1, st))

    def even_tail(st):
        return last_steps(s_a, st)

    l_fin, acc_fin = lax.cond(qi % 2 == 1, odd_tail, even_tail, state)

    for h in heads:
        o = acc_fin[h] / l_fin[h]
        o = o[:, :t] - lam * o[:, t:]
        ms = jnp.mean(o * o, axis=0, keepdims=True)
        o = (o * lax.rsqrt(ms + EPS)).T
        o_ref[:, h * HEAD_W:(h + 1) * HEAD_W] = (o * og_ref[...] * (1.0 - LAMBDA_INIT)).astype(bf16)


def _diff_attention(proj, proj_meta, lq1, lk1, lq2, lk2, og, *, t):
    vec = lambda n: pl.BlockSpec((1, n), lambda b, h, i: (0, 0))
    return _attn_call(
        _diff_kernel, "diff_attn",
        [vec(QK_DIM), vec(QK_DIM), vec(QK_DIM), vec(QK_DIM), vec(HEAD_W)], (lq1, lk1, lq2, lk2, og),
        proj, proj_meta, _attn_specs(proj.shape[2], t, DIFF_HEADS_PER_STEP, 0, N_HEADS, 2 * N_HEADS),
        [pltpu.VMEM((DIFF_HEADS_PER_STEP, META_PAD, 2 * t), f32),
         pltpu.VMEM((DIFF_HEADS_PER_STEP, t, 2 * t), f32),
         pltpu.VMEM((DIFF_HEADS_PER_STEP, t, 2 * t), f32)],
        t=t, hps=DIFF_HEADS_PER_STEP)


def _sb_kernel(og_ref, q_ref, k_ref, v_ref, km_ref, vm_ref, o_ref, d_meta, d_a, d_b, *, t, hps):
    qi = pl.program_id(2)
    heads = range(hps)
    nt_dims = (((1,), (1,)), ((), ()))
    tn_dims = (((0,), (0,)), ((), ()))
    qh = [q_ref[0, h] for h in heads]

    def later_key_matrix(n):
        r = lax.broadcasted_iota(jnp.int32, (n, n), 0)
        c = lax.broadcasted_iota(jnp.int32, (n, n), 1)
        return (c > r).astype(bf16)

    def scores(kblks, valid):
        zs = [lax.dot_general(kblks[h], qh[h], nt_dims, preferred_element_type=f32) for h in heads]
        if valid is not None:
            zs = [jnp.where(valid, z, NEG_INF) for z in zs]
        return zs

    def produce(buf, zs, lmat):
        nsub = lmat.shape[0]
        subs = range(zs[0].shape[0] // nsub)
        sps, laters = [], []
        for h in heads:
            sp = jnp.maximum(zs[h], 0.0) + jnp.log(1.0 + jnp.exp2(-jnp.abs(zs[h]))) * LOG2E
            buf[h] = zs[h] - sp
            sps.append([sp[i * nsub:i * nsub + 1, :] for i in subs])
            laters.append([jnp.dot(lmat, sp[i * nsub:(i + 1) * nsub].astype(bf16), preferred_element_type=f32)
                           for i in subs])
        totals = []
        for h in heads:
            after = None
            for i in reversed(subs):
                rows = slice(i * nsub, (i + 1) * nsub)
                later = laters[h][i] if after is None else laters[h][i] + after
                buf[h, rows] = buf[h, rows] - later
                sub_total = laters[h][i][0:1, :] + sps[h][i]
                after = sub_total if after is None else after + sub_total
            totals.append(after)
        return totals

    def consume(buf, vblks, run, acc):
        a = [jnp.exp2(buf[h] - run[h]).astype(bf16) for h in heads]
        return [acc[h] + lax.dot_general(vblks[h], a[h], tn_dims, preferred_element_type=f32) for h in heads]

    def k_block(j):
        start = pl.multiple_of(j * t, t)
        return [k_ref[0, h, pl.ds(start, t), :] for h in heads]

    def v_block(j):
        start = pl.multiple_of(j * t, t)
        return [v_ref[0, h, pl.ds(start, t), :] for h in heads]

    lmat_t = later_key_matrix(min(t, SB_CUMSUM_BLOCK))

    def step(cur, nxt, j, state):
        run_c, run_n, acc = state
        zs = scores(k_block(j - 1), None)
        acc = consume(cur, v_block(j), run_c, acc)
        totals = produce(nxt, zs, lmat_t)
        return (run_n, [run_n[h] + totals[h] for h in heads], acc)

    def last_steps(cur, state):
        run_c, run_n, acc = state
        zs = scores([km_ref[0, h] for h in heads], lax.broadcasted_iota(jnp.int32, (META_PAD, t), 0) < N_META)
        acc = consume(cur, v_block(0), run_c, acc)
        produce(d_meta, zs, later_key_matrix(META_PAD))
        return consume(d_meta, [vm_ref[0, h] for h in heads], run_n, acc)

    key = lax.broadcasted_iota(jnp.int32, (t, t), 0)
    qry = lax.broadcasted_iota(jnp.int32, (t, t), 1)
    totals = produce(d_a, scores(k_block(qi), key < qry), lmat_t)
    state = ([jnp.zeros((1, t), f32) for _ in heads], totals, [jnp.zeros((HEAD_W, t), f32) for _ in heads])

    def pair(n, st):
        st = step(d_a, d_b, qi - 2 * n, st)
        return step(d_b, d_a, qi - 2 * n - 1, st)

    def alive(st):
        lowest = st[0][0]
        for h in heads[1:]:
            lowest = jnp.minimum(lowest, st[0][h])
        return jnp.min(lowest) < SB_DEAD_LOG2

    def pair_while(carry):
        n, _, st = carry
        st = pair(n, st)
        return n + 1, alive(st), st

    _, still_alive, state = lax.while_loop(
        lambda carry: jnp.logical_and(carry[0] < qi // 2, carry[1]), pair_while, (jnp.int32(0), True, state))

    def odd_tail(st):
        return last_steps(d_b, step(d_a, d_b, 1, st))

    def even_tail(st):
        return last_steps(d_a, st)

    def remaining(st):
        return lax.cond(qi % 2 == 1, odd_tail, even_tail, st)

    acc_fin = lax.cond(still_alive, remaining, lambda st: st[2], state)
    for h in heads:
        acc = acc_fin[h]
        ms = jnp.mean(acc * acc, axis=0, keepdims=True)
        o = (acc * lax.rsqrt(ms + EPS)).T
        o_ref[:, h * HEAD_W:(h + 1) * HEAD_W] = (o * og_ref[...]).astype(bf16)


def _sb_attention(proj, proj_meta, og, *, t):
    return _attn_call(
        _sb_kernel, "sb_attn", [pl.BlockSpec((1, HEAD_W), lambda b, h, i: (0, 0))], (og,),
        proj, proj_meta, _attn_specs(proj.shape[2], t, SB_HEADS_PER_STEP, 3 * N_HEADS, 4 * N_HEADS, 5 * N_HEADS),
        [pltpu.VMEM((SB_HEADS_PER_STEP, META_PAD, t), f32),
         pltpu.VMEM((SB_HEADS_PER_STEP, t, t), f32),
         pltpu.VMEM((SB_HEADS_PER_STEP, t, t), f32)],
        t=t, hps=SB_HEADS_PER_STEP)


def _outproj_kernel(x_ref, md_ref, ms_ref, wd_ref, ws_ref, o_ref):
    o_ref[...] = (x_ref[...]
                  + jnp.dot(md_ref[...], wd_ref[...], preferred_element_type=f32)
                  + jnp.dot(ms_ref[...], ws_ref[...], preferred_element_type=f32))


def _outproj(x2d, mixed_diff, mixed_sb, w_out, *, tm):
    rows, d = x2d.shape
    half = mixed_diff.shape[1]
    return pl.pallas_call(
        _outproj_kernel,
        grid=(rows // tm,),
        in_specs=[
            pl.BlockSpec((tm, d), lambda i: (i, 0)),
            pl.BlockSpec((tm, half), lambda i: (i, 0)),
            pl.BlockSpec((tm, half), lambda i: (i, 0)),
            pl.BlockSpec((half, d), lambda i: (0, 0)),
            pl.BlockSpec((half, d), lambda i: (1, 0)),
        ],
        out_specs=pl.BlockSpec((tm, d), lambda i: (i, 0)),
        out_shape=jax.ShapeDtypeStruct((rows, d), f32),
        compiler_params=pltpu.CompilerParams(
            dimension_semantics=("parallel",), vmem_limit_bytes=VMEM_LIMIT),
        name="outproj",
    )(x2d, mixed_diff, mixed_sb, w_out, w_out)


def _mlp_kernel(h_ref, g_ref, wu_ref, wd_ref, o_ref, m_ref):
    @pl.when(pl.program_id(1) == 0)
    def _():
        h = h_ref[...]
        ms = jnp.mean(h * h, axis=-1, keepdims=True)
        m_ref[...] = (h * lax.rsqrt(ms + EPS) * g_ref[...]).astype(bf16)
        o_ref[...] = h

    hid = jnp.dot(m_ref[...], wu_ref[...], preferred_element_type=f32)
    hid = jnp.square(jnp.maximum(hid, 0.0))
    o_ref[...] += jnp.dot(hid.astype(bf16), wd_ref[...], preferred_element_type=f32)


def _mlp(h1, g_mlp, w_up, w_down, *, tm, tf):
    rows, d = h1.shape
    return pl.pallas_call(
        _mlp_kernel,
        grid=(rows // tm, w_up.shape[1] // tf),
        in_specs=[
            pl.BlockSpec((tm, d), lambda i, f: (i, 0)),
            pl.BlockSpec((1, d), lambda i, f: (0, 0)),
            pl.BlockSpec((d, tf), lambda i, f: (0, f)),
            pl.BlockSpec((tf, d), lambda i, f: (f, 0)),
        ],
        out_specs=pl.BlockSpec((tm, d), lambda i, f: (i, 0)),
        out_shape=jax.ShapeDtypeStruct((rows, d), f32),
        scratch_shapes=[pltpu.VMEM((tm, d), bf16)],
        compiler_params=pltpu.CompilerParams(
            dimension_semantics=("parallel", "arbitrary"), vmem_limit_bytes=VMEM_LIMIT),
        name="mlp",
    )(h1, g_mlp, w_up, w_down)


def _rope_tables(n_pos):
    pos = np.arange(n_pos, dtype=np.float64)
    inv = ROPE_THETA ** (-np.arange(0, QK_DIM, 2, dtype=np.float64) / QK_DIM)
    ang = pos[:, None] * inv[None, :]
    cos, sin = np.cos(ang).astype(np.float32), np.sin(ang).astype(np.float32)
    return (jnp.asarray(np.concatenate([cos, cos, cos, cos], axis=-1)),
            jnp.asarray(np.concatenate([-sin, sin, -sin, sin], axis=-1)))


def kernel(x, meta_tokens, g_mix, w_in, q_norm_g, k_norm_g, lambda_q1, lambda_k1, lambda_q2, lambda_k2,
           diff_out_g, sb_out_g, w_out, g_mlp, w_up, w_down):
    batch, seq, d = x.shape
    assert g_mix.shape[0] == 1, "single-layer kernel"
    assert meta_tokens.shape[0] == N_META and seq % DIFF_BLOCK == 0 and seq % SB_BLOCK == 0

    x2d = x.reshape(batch * seq, d)
    w_in_b = w_in[0].astype(bf16)
    qg = jnp.tile(q_norm_g[0], 2)[None, :]
    kg = jnp.tile(k_norm_g[0], 2)[None, :]
    cos_t, sin_t = _rope_tables(N_META + seq)

    proj, w_up_b, w_down_b, w_out_b = _inproj(x2d, g_mix, w_in_b, cos_t[N_META:], sin_t[N_META:], qg, kg,
                                              batch=batch, seq=seq, tm=INPROJ_ROWS, tn=INPROJ_COLS,
                                              cast=((w_up[0], 1), (w_down[0], 0), (w_out[0], 0)))
    proj_meta = _inproj(meta_tokens.astype(f32), g_mix, w_in_b, cos_t[:N_META], sin_t[:N_META], qg, kg,
                        batch=1, seq=N_META, tm=N_META, tn=INPROJ_COLS)
    proj_meta = jnp.pad(proj_meta, ((0, 0), (0, 0), (0, META_PAD - N_META), (0, 0)))

    mixed_diff = _diff_attention(proj, proj_meta, lambda_q1, lambda_k1, lambda_q2, lambda_k2, diff_out_g,
                                 t=DIFF_BLOCK)
    mixed_sb = _sb_attention(proj, proj_meta, sb_out_g, t=SB_BLOCK)

    h1 = _outproj(x2d, mixed_diff, mixed_sb, w_out_b, tm=OUTPROJ_ROWS)
    out = _mlp(h1, g_mlp, w_up_b, w_down_b, tm=MLP_ROWS, tf=MLP_HIDDEN_COLS)
    return out.reshape(batch, seq, d)
```

```python
import functools
import math

import jax
import jax.numpy as jnp
import numpy as np
from jax import lax
from jax.experimental import pallas as pl
from jax.experimental.pallas import tpu as pltpu

N_META = 16
N_HEADS = 8
HEAD_W = 128
QK_DIM = 64
N_GROUPS = 6 * N_HEADS
ROPE_THETA = 10000.0
EPS = 1e-6
NEG_INF = -1e30
LAMBDA_INIT = 0.8 - 0.6 * math.exp(0.0)
LOG2E = math.log2(math.e)

INPROJ_ROWS, INPROJ_COLS = 256, 512
OUTPROJ_ROWS = 512
MLP_ROWS, MLP_HIDDEN_COLS = 512, 1024
META_PAD = 128
DIFF_BLOCK = 512
SB_BLOCK = 256
SB_CUMSUM_BLOCK = 256
SB_DEAD_LOG2 = 160.0
DIFF_HEADS_PER_STEP = 4
SB_HEADS_PER_STEP = 8
VMEM_LIMIT = 56 * 1024 * 1024

f32 = jnp.float32
bf16 = jnp.bfloat16


def _lane_iota(shape):
    return lax.broadcasted_iota(jnp.int32, shape, len(shape) - 1)


def _inproj_kernel(x_ref, g_ref, w_ref, cos_ref, sin_ref, qg_ref, kg_ref, *refs, tn):
    n_cast = (len(refs) - 1) // 2
    o_ref = refs[n_cast]
    for src_ref, dst_ref in zip(refs[:n_cast], refs[n_cast + 1:]):
        dst_ref[...] = src_ref[...].astype(bf16)

    xf = x_ref[...]
    ms = jnp.mean(xf * xf, axis=-1, keepdims=True)
    u = (xf * lax.rsqrt(ms + EPS) * g_ref[...]).astype(bf16)

    ngrp = tn // HEAD_W
    lane = _lane_iota((1, HEAD_W))
    first = lane < QK_DIM
    low_half = (lane & (QK_DIM // 2)) == 0
    cos = cos_ref[...]
    sin = sin_ref[...]

    for j in range(w_ref.shape[1] // tn):
        acc = jnp.dot(u, w_ref[:, j * tn:(j + 1) * tn], preferred_element_type=f32)
        for g in range(ngrp):
            grp = j * ngrp + g
            t = acc[:, g * HEAD_W:(g + 1) * HEAD_W]
            if grp < 2 * N_HEADS:
                is_q = grp < N_HEADS
                ss = t * t
                tot = jnp.sum(ss, axis=-1, keepdims=True)
                lo = jnp.sum(jnp.where(first, ss, 0.0), axis=-1, keepdims=True)
                msq = jnp.where(first, lo, tot - lo) * (1.0 / QK_DIM)
                y = t * lax.rsqrt(msq + EPS) * (qg_ref[...] if is_q else kg_ref[...])
                rot = jnp.where(low_half, pltpu.roll(y, HEAD_W - QK_DIM // 2, 1), pltpu.roll(y, QK_DIM // 2, 1))
                t = y * cos + rot * sin
                if is_q:
                    t = t * (QK_DIM ** -0.5 * LOG2E)
            elif 3 * N_HEADS <= grp < 4 * N_HEADS:
                t = t * (HEAD_W ** -0.5 * LOG2E)
            o_ref[0, grp] = t.astype(bf16)


def _inproj(x2d, g_mix, w_in, cos_t, sin_t, qg, kg, *, batch, seq, tm, tn, cast=()):
    rows, d = x2d.shape
    nb = seq // tm
    steps = rows // tm

    def cast_specs():
        specs = []
        for arr, axis in cast:
            blk = tuple(n // steps if a == axis else n for a, n in enumerate(arr.shape))
            specs.append(pl.BlockSpec(blk, (lambda i: (i, 0)) if axis == 0 else (lambda i: (0, i))))
        return specs

    outs = pl.pallas_call(
        functools.partial(_inproj_kernel, tn=tn),
        grid=(rows // tm,),
        in_specs=[
            pl.BlockSpec((tm, d), lambda i: (i, 0)),
            pl.BlockSpec((1, d), lambda i: (0, 0)),
            pl.BlockSpec(w_in.shape, lambda i: (0, 0), pipeline_mode=pl.Buffered(1)),
            pl.BlockSpec((tm, HEAD_W), lambda i: (i % nb, 0)),
            pl.BlockSpec((tm, HEAD_W), lambda i: (i % nb, 0)),
            pl.BlockSpec((1, HEAD_W), lambda i: (0, 0)),
            pl.BlockSpec((1, HEAD_W), lambda i: (0, 0)),
        ] + cast_specs(),
        out_specs=[pl.BlockSpec((1, N_GROUPS, tm, HEAD_W), lambda i: (i // nb, 0, i % nb, 0))] + cast_specs(),
        out_shape=[jax.ShapeDtypeStruct((batch, N_GROUPS, seq, HEAD_W), bf16)]
        + [jax.ShapeDtypeStruct(arr.shape, bf16) for arr, _ in cast],
        compiler_params=pltpu.CompilerParams(dimension_semantics=("parallel",), vmem_limit_bytes=VMEM_LIMIT),
        name="inproj",
    )(x2d, g_mix, w_in, cos_t, sin_t, qg, kg, *[arr for arr, _ in cast])
    return outs if cast else outs[0]


def _attn_specs(seq, t, hps, q_grp, k_grp, v_grp):
    return [
        pl.BlockSpec((1, hps, t, HEAD_W), lambda b, h, i: (b, q_grp // hps + h, i, 0)),
        pl.BlockSpec((1, hps, seq, HEAD_W), lambda b, h, i: (b, k_grp // hps + h, 0, 0)),
        pl.BlockSpec((1, hps, seq, HEAD_W), lambda b, h, i: (b, v_grp // hps + h, 0, 0)),
        pl.BlockSpec((1, hps, META_PAD, HEAD_W), lambda b, h, i: (0, k_grp // hps + h, 0, 0)),
        pl.BlockSpec((1, hps, META_PAD, HEAD_W), lambda b, h, i: (0, v_grp // hps + h, 0, 0)),
    ]


def _attn_call(kernel_fn, name, small_specs, small_args, proj, proj_meta, grp_specs, scratch, *, t, hps):
    batch, _, seq, _ = proj.shape
    nq = seq // t
    return pl.pallas_call(
        functools.partial(kernel_fn, t=t, hps=hps),
        grid=(batch, N_HEADS // hps, nq),
        in_specs=small_specs + grp_specs,
        out_specs=pl.BlockSpec((t, hps * HEAD_W), lambda b, h, i: (b * nq + i, h)),
        out_shape=jax.ShapeDtypeStruct((batch * seq, N_HEADS * HEAD_W), bf16),
        scratch_shapes=scratch,
        compiler_params=pltpu.CompilerParams(
            dimension_semantics=("parallel", "parallel", "arbitrary"), vmem_limit_bytes=VMEM_LIMIT),
        name=name,
    )(*small_args, proj, proj, proj, proj_meta, proj_meta)


def _diff_kernel(lq1_ref, lk1_ref, lq2_ref, lk2_ref, og_ref, q_ref, k_ref, v_ref, km_ref, vm_ref, o_ref,
                 s_meta, s_a, s_b, *, t, hps):
    qi = pl.program_id(2)
    heads = range(hps)
    lane_q = _lane_iota((t, HEAD_W))
    nt_dims = (((1,), (1,)), ((), ()))
    tn_dims = (((0,), (0,)), ((), ()))

    def stacked_q(h):
        q = q_ref[0, h]
        zero = jnp.zeros_like(q)
        return jnp.concatenate([jnp.where(lane_q < QK_DIM, q, zero), jnp.where(lane_q >= QK_DIM, q, zero)], axis=0)

    qs = [stacked_q(h) for h in heads]

    lam = (jnp.exp(jnp.sum(lq1_ref[...] * lk1_ref[...], axis=-1, keepdims=True))
           - jnp.exp(jnp.sum(lq2_ref[...] * lk2_ref[...], axis=-1, keepdims=True))
           + LAMBDA_INIT)

    key = lax.broadcasted_iota(jnp.int32, (t, 2 * t), 0)
    qry = lax.broadcasted_iota(jnp.int32, (t, 2 * t), 1)
    causal = key <= jnp.where(qry >= t, qry - t, qry)

    def produce(buf, kblks, mask, m_in):
        m_out = []
        for h in heads:
            s = lax.dot_general(kblks[h], qs[h], nt_dims, preferred_element_type=f32)
            if mask is not None:
                s = jnp.where(mask, s, NEG_INF)
            buf[h] = s
            m_out.append(jnp.maximum(m_in[h], jnp.max(s, axis=0, keepdims=True)))
        return m_out

    def k_block(j):
        start = pl.multiple_of(j * t, t)
        return [k_ref[0, h, pl.ds(start, t), :] for h in heads]

    def consume(buf, vblks, state):
        m_prev, m_cur, l, acc = state
        ls, accs = [], []
        for h in heads:
            alpha = jnp.exp2(m_prev[h] - m_cur[h])
            p = jnp.exp2(buf[h] - m_cur[h])
            ls.append(alpha * l[h] + jnp.sum(p, axis=0, keepdims=True))
            accs.append(alpha * acc[h]
                        + lax.dot_general(vblks[h], p.astype(bf16), tn_dims, preferred_element_type=f32))
        return ls, accs

    def v_block(j):
        start = pl.multiple_of(j * t, t)
        return [v_ref[0, h, pl.ds(start, t), :] for h in heads]

    def step(cur, nxt, j, state):
        m_next = produce(nxt, k_block(j - 1), None, state[1])
        l, acc = consume(cur, v_block(j), state)
        return (state[1], m_next, l, acc)

    def last_steps(cur, state):
        meta_mask = lax.broadcasted_iota(jnp.int32, (META_PAD, 2 * t), 0) < N_META
        m_meta = produce(s_meta, [km_ref[0, h] for h in heads], meta_mask, state[1])
        l, acc = consume(cur, v_block(0), state)
        return consume(s_meta, [vm_ref[0, h] for h in heads], (state[1], m_meta, l, acc))

    neg = [jnp.full((1, 2 * t), NEG_INF, f32) for _ in heads]
    m_diag = produce(s_a, k_block(qi), causal, neg)
    state = (neg, m_diag, [jnp.zeros((1, 2 * t), f32) for _ in heads],
             [jnp.zeros((HEAD_W, 2 * t), f32) for _ in heads])

    def pair(n, st):
        st = step(s_a, s_b, qi - 2 * n, st)
        return step(s_b, s_a, qi - 2 * n - 1, st)

    state = lax.fori_loop(0, qi // 2, pair, state)

    def odd_tail(st):
        return last_steps(s_b, step(s_a, s_b, 1, st))

    def even_tail(st):
        return last_steps(s_a, st)

    l_fin, acc_fin = lax.cond(qi % 2 == 1, odd_tail, even_tail, state)

    for h in heads:
        o = acc_fin[h] / l_fin[h]
        o = o[:, :t] - lam * o[:, t:]
        ms = jnp.mean(o * o, axis=0, keepdims=True)
        o = (o * lax.rsqrt(ms + EPS)).T
        o_ref[:, h * HEAD_W:(h + 1) * HEAD_W] = (o * og_ref[...] * (1.0 - LAMBDA_INIT)).astype(bf16)


def _diff_attention(proj, proj_meta, lq1, lk1, lq2, lk2, og, *, t):
    vec = lambda n: pl.BlockSpec((1, n), lambda b, h, i: (0, 0))
    return _attn_call(
        _diff_kernel, "diff_attn",
        [vec(QK_DIM), vec(QK_DIM), vec(QK_DIM), vec(QK_DIM), vec(HEAD_W)], (lq1, lk1, lq2, lk2, og),
        proj, proj_meta, _attn_specs(proj.shape[2], t, DIFF_HEADS_PER_STEP, 0, N_HEADS, 2 * N_HEADS),
        [pltpu.VMEM((DIFF_HEADS_PER_STEP, META_PAD, 2 * t), f32),
         pltpu.VMEM((DIFF_HEADS_PER_STEP, t, 2 * t), f32),
         pltpu.VMEM((DIFF_HEADS_PER_STEP, t, 2 * t), f32)],
        t=t, hps=DIFF_HEADS_PER_STEP)


def _sb_kernel(og_ref, q_ref, k_ref, v_ref, km_ref, vm_ref, o_ref, d_meta, d_ab, *, t, hps):
    qi = pl.program_id(2)
    heads = range(hps)
    nt_dims = (((1,), (1,)), ((), ()))
    tn_dims = (((0,), (0,)), ((), ()))
    qh = [q_ref[0, h] for h in heads]

    def later_key_matrix(n):
        r = lax.broadcasted_iota(jnp.int32, (n, n), 0)
        c = lax.broadcasted_iota(jnp.int32, (n, n), 1)
        return (c > r).astype(bf16)

    def scores(kblks, valid):
        zs = [lax.dot_general(kblks[h], qh[h], nt_dims, preferred_element_type=f32) for h in heads]
        if valid is not None:
            zs = [jnp.where(valid, z, NEG_INF) for z in zs]
        return zs

    def produce(buf, zs, lmat):
        nsub = lmat.shape[0]
        subs = range(zs[0].shape[0] // nsub)
        sps, laters = [], []
        for h in heads:
            sp = jnp.maximum(zs[h], 0.0) + jnp.log(1.0 + jnp.exp2(-jnp.abs(zs[h]))) * LOG2E
            buf[h] = zs[h] - sp
            sps.append([sp[i * nsub:i * nsub + 1, :] for i in subs])
            laters.append([jnp.dot(lmat, sp[i * nsub:(i + 1) * nsub].astype(bf16), preferred_element_type=f32)
                           for i in subs])
        totals = []
        for h in heads:
            after = None
            for i in reversed(subs):
                rows = slice(i * nsub, (i + 1) * nsub)
                later = laters[h][i] if after is None else laters[h][i] + after
                buf[h, rows] = buf[h, rows] - later
                sub_total = laters[h][i][0:1, :] + sps[h][i]
                after = sub_total if after is None else after + sub_total
            totals.append(after)
        return totals

    def consume(buf, vblks, run, acc):
        a = [jnp.exp2(buf[h] - run[h]).astype(bf16) for h in heads]
        return [acc[h] + lax.dot_general(vblks[h], a[h], tn_dims, preferred_element_type=f32) for h in heads]

    def k_block(j):
        start = pl.multiple_of(j * t, t)
        return [k_ref[0, h, pl.ds(start, t), :] for h in heads]

    def v_block(j):
        start = pl.multiple_of(j * t, t)
        return [v_ref[0, h, pl.ds(start, t), :] for h in heads]

    lmat_t = later_key_matrix(min(t, SB_CUMSUM_BLOCK))

    def step(cur, nxt, j, state):
        run_c, run_n, acc = state
        zs = scores(k_block(j - 1), None)
        acc = consume(cur, v_block(j), run_c, acc)
        totals = produce(nxt, zs, lmat_t)
        return (run_n, [run_n[h] + totals[h] for h in heads], acc)

    def meta_block(run, acc):
        zs = scores([km_ref[0, h] for h in heads], lax.broadcasted_iota(jnp.int32, (META_PAD, t), 0) < N_META)
        produce(d_meta, zs, later_key_matrix(META_PAD))
        return consume(d_meta, [vm_ref[0, h] for h in heads], run, acc)

    def alive(run):
        lowest = run[0]
        for h in heads[1:]:
            lowest = jnp.minimum(lowest, run[h])
        return jnp.min(lowest) < SB_DEAD_LOG2

    key = lax.broadcasted_iota(jnp.int32, (t, t), 0)
    qry = lax.broadcasted_iota(jnp.int32, (t, t), 1)
    totals = produce(d_ab.at[0], scores(k_block(qi), key < qry), lmat_t)
    state = ([jnp.zeros((1, t), f32) for _ in heads], totals, [jnp.zeros((HEAD_W, t), f32) for _ in heads])

    def one_step(carry):
        p, _, st = carry
        st = step(d_ab.at[p % 2], d_ab.at[(p + 1) % 2], qi - p, st)
        return p + 1, alive(st[1]), st

    p, _, (run_c, run_n, acc) = lax.while_loop(
        lambda carry: jnp.logical_and(carry[0] < qi, carry[1]), one_step, (jnp.int32(0), alive(totals), state))

    acc = lax.cond(alive(run_c), lambda a: consume(d_ab.at[p % 2], v_block(qi - p), run_c, a), lambda a: a, acc)
    acc_fin = lax.cond(jnp.logical_and(p == qi, alive(run_n)), lambda a: meta_block(run_n, a), lambda a: a, acc)
    for h in heads:
        acc = acc_fin[h]
        ms = jnp.mean(acc * acc, axis=0, keepdims=True)
        o = (acc * lax.rsqrt(ms + EPS)).T
        o_ref[:, h * HEAD_W:(h + 1) * HEAD_W] = (o * og_ref[...]).astype(bf16)


def _sb_attention(proj, proj_meta, og, *, t):
    return _attn_call(
        _sb_kernel, "sb_attn", [pl.BlockSpec((1, HEAD_W), lambda b, h, i: (0, 0))], (og,),
        proj, proj_meta, _attn_specs(proj.shape[2], t, SB_HEADS_PER_STEP, 3 * N_HEADS, 4 * N_HEADS, 5 * N_HEADS),
        [pltpu.VMEM((SB_HEADS_PER_STEP, META_PAD, t), f32),
         pltpu.VMEM((2, SB_HEADS_PER_STEP, t, t), f32)],
        t=t, hps=SB_HEADS_PER_STEP)


def _outproj_kernel(x_ref, md_ref, ms_ref, wd_ref, ws_ref, o_ref):
    o_ref[...] = (x_ref[...]
                  + jnp.dot(md_ref[...], wd_ref[...], preferred_element_type=f32)
                  + jnp.dot(ms_ref[...], ws_ref[...], preferred_element_type=f32))


def _outproj(x2d, mixed_diff, mixed_sb, w_out, *, tm):
    rows, d = x2d.shape
    half = mixed_diff.shape[1]
    return pl.pallas_call(
        _outproj_kernel,
        grid=(rows // tm,),
        in_specs=[
            pl.BlockSpec((tm, d), lambda i: (i, 0)),
            pl.BlockSpec((tm, half), lambda i: (i, 0)),
            pl.BlockSpec((tm, half), lambda i: (i, 0)),
            pl.BlockSpec((half, d), lambda i: (0, 0)),
            pl.BlockSpec((half, d), lambda i: (1, 0)),
        ],
        out_specs=pl.BlockSpec((tm, d), lambda i: (i, 0)),
        out_shape=jax.ShapeDtypeStruct((rows, d), f32),
        compiler_params=pltpu.CompilerParams(
            dimension_semantics=("parallel",), vmem_limit_bytes=VMEM_LIMIT),
        name="outproj",
    )(x2d, mixed_diff, mixed_sb, w_out, w_out)


def _mlp_kernel(h_ref, g_ref, wu_ref, wd_ref, o_ref, m_ref):
    @pl.when(pl.program_id(1) == 0)
    def _():
        h = h_ref[...]
        ms = jnp.mean(h * h, axis=-1, keepdims=True)
        m_ref[...] = (h * lax.rsqrt(ms + EPS) * g_ref[...]).astype(bf16)
        o_ref[...] = h

    hid = jnp.dot(m_ref[...], wu_ref[...], preferred_element_type=f32)
    hid = jnp.square(jnp.maximum(hid, 0.0))
    o_ref[...] += jnp.dot(hid.astype(bf16), wd_ref[...], preferred_element_type=f32)


def _mlp(h1, g_mlp, w_up, w_down, *, tm, tf):
    rows, d = h1.shape
    return pl.pallas_call(
        _mlp_kernel,
        grid=(rows // tm, w_up.shape[1] // tf),
        in_specs=[
            pl.BlockSpec((tm, d), lambda i, f: (i, 0)),
            pl.BlockSpec((1, d), lambda i, f: (0, 0)),
            pl.BlockSpec((d, tf), lambda i, f: (0, f)),
            pl.BlockSpec((tf, d), lambda i, f: (f, 0)),
        ],
        out_specs=pl.BlockSpec((tm, d), lambda i, f: (i, 0)),
        out_shape=jax.ShapeDtypeStruct((rows, d), f32),
        scratch_shapes=[pltpu.VMEM((tm, d), bf16)],
        compiler_params=pltpu.CompilerParams(
            dimension_semantics=("parallel", "arbitrary"), vmem_limit_bytes=VMEM_LIMIT),
        name="mlp",
    )(h1, g_mlp, w_up, w_down)


def _rope_tables(n_pos):
    pos = np.arange(n_pos, dtype=np.float64)
    inv = ROPE_THETA ** (-np.arange(0, QK_DIM, 2, dtype=np.float64) / QK_DIM)
    ang = pos[:, None] * inv[None, :]
    cos, sin = np.cos(ang).astype(np.float32), np.sin(ang).astype(np.float32)
    return (jnp.asarray(np.concatenate([cos, cos, cos, cos], axis=-1)),
            jnp.asarray(np.concatenate([-sin, sin, -sin, sin], axis=-1)))


def kernel(x, meta_tokens, g_mix, w_in, q_norm_g, k_norm_g, lambda_q1, lambda_k1, lambda_q2, lambda_k2,
           diff_out_g, sb_out_g, w_out, g_mlp, w_up, w_down):
    batch, seq, d = x.shape
    assert g_mix.shape[0] == 1, "single-layer kernel"
    assert meta_tokens.shape[0] == N_META and seq % DIFF_BLOCK == 0 and seq % SB_BLOCK == 0

    x2d = x.reshape(batch * seq, d)
    w_in_b = w_in[0].astype(bf16)
    qg = jnp.tile(q_norm_g[0], 2)[None, :]
    kg = jnp.tile(k_norm_g[0], 2)[None, :]
    cos_t, sin_t = _rope_tables(N_META + seq)

    proj, w_up_b, w_down_b, w_out_b = _inproj(x2d, g_mix, w_in_b, cos_t[N_META:], sin_t[N_META:], qg, kg,
                                              batch=batch, seq=seq, tm=INPROJ_ROWS, tn=INPROJ_COLS,
                                              cast=((w_up[0], 1), (w_down[0], 0), (w_out[0], 0)))
    proj_meta = _inproj(meta_tokens.astype(f32), g_mix, w_in_b, cos_t[:N_META], sin_t[:N_META], qg, kg,
                        batch=1, seq=N_META, tm=N_META, tn=INPROJ_COLS)
    proj_meta = jnp.pad(proj_meta, ((0, 0), (0, 0), (0, META_PAD - N_META), (0, 0)))

    mixed_diff = _diff_attention(proj, proj_meta, lambda_q1, lambda_k1, lambda_q2, lambda_k2, diff_out_g,
                                 t=DIFF_BLOCK)
    mixed_sb = _sb_attention(proj, proj_meta, sb_out_g, t=SB_BLOCK)

    h1 = _outproj(x2d, mixed_diff, mixed_sb, w_out_b, tm=OUTPROJ_ROWS)
    out = _mlp(h1, g_mlp, w_up_b, w_down_b, tm=MLP_ROWS, tf=MLP_HIDDEN_COLS)
    return out.reshape(batch, seq, d)
```

```python
import functools
import math

import jax
import jax.numpy as jnp
import numpy as np
from jax import lax
from jax.experimental import pallas as pl
from jax.experimental.pallas import tpu as pltpu

N_META = 16
N_HEADS = 8
HEAD_W = 128
QK_DIM = 64
N_GROUPS = 6 * N_HEADS
ROPE_THETA = 10000.0
EPS = 1e-6
NEG_INF = -1e30
LAMBDA_INIT = 0.8 - 0.6 * math.exp(0.0)
LOG2E = math.log2(math.e)

INPROJ_ROWS, INPROJ_COLS = 256, 512
OUTPROJ_ROWS = 512
MLP_ROWS, MLP_HIDDEN_COLS = 512, 1024
META_PAD = 128
DIFF_BLOCK = 512
SB_BLOCK = 256
SB_CUMSUM_BLOCK = 256
SB_DEAD_LOG2 = 160.0
DIFF_HEADS_PER_STEP = 4
SB_HEADS_PER_STEP = 8
VMEM_LIMIT = 56 * 1024 * 1024

f32 = jnp.float32
bf16 = jnp.bfloat16


def _lane_iota(shape):
    return lax.broadcasted_iota(jnp.int32, shape, len(shape) - 1)


def _inproj_kernel(x_ref, g_ref, w_ref, cos_ref, sin_ref, qg_ref, kg_ref, *refs, tn):
    n_cast = (len(refs) - 1) // 2
    o_ref = refs[n_cast]
    for src_ref, dst_ref in zip(refs[:n_cast], refs[n_cast + 1:]):
        dst_ref[...] = src_ref[...].astype(bf16)

    xf = x_ref[...]
    ms = jnp.mean(xf * xf, axis=-1, keepdims=True)
    u = (xf * lax.rsqrt(ms + EPS) * g_ref[...]).astype(bf16)

    ngrp = tn // HEAD_W
    lane = _lane_iota((1, HEAD_W))
    first = lane < QK_DIM
    low_half = (lane & (QK_DIM // 2)) == 0
    cos = cos_ref[...]
    sin = sin_ref[...]

    for j in range(w_ref.shape[1] // tn):
        acc = jnp.dot(u, w_ref[:, j * tn:(j + 1) * tn], preferred_element_type=f32)
        for g in range(ngrp):
            grp = j * ngrp + g
            t = acc[:, g * HEAD_W:(g + 1) * HEAD_W]
            if grp < 2 * N_HEADS:
                is_q = grp < N_HEADS
                ss = t * t
                tot = jnp.sum(ss, axis=-1, keepdims=True)
                lo = jnp.sum(jnp.where(first, ss, 0.0), axis=-1, keepdims=True)
                msq = jnp.where(first, lo, tot - lo) * (1.0 / QK_DIM)
                y = t * lax.rsqrt(msq + EPS) * (qg_ref[...] if is_q else kg_ref[...])
                rot = jnp.where(low_half, pltpu.roll(y, HEAD_W - QK_DIM // 2, 1), pltpu.roll(y, QK_DIM // 2, 1))
                t = y * cos + rot * sin
                if is_q:
                    t = t * (QK_DIM ** -0.5 * LOG2E)
            elif 3 * N_HEADS <= grp < 4 * N_HEADS:
                t = t * (HEAD_W ** -0.5 * LOG2E)
            o_ref[0, grp] = t.astype(bf16)


def _inproj(x2d, g_mix, w_in, cos_t, sin_t, qg, kg, *, batch, seq, tm, tn, cast=()):
    rows, d = x2d.shape
    nb = seq // tm
    steps = rows // tm

    def cast_specs():
        specs = []
        for arr, axis in cast:
            blk = tuple(n // steps if a == axis else n for a, n in enumerate(arr.shape))
            specs.append(pl.BlockSpec(blk, (lambda i: (i, 0)) if axis == 0 else (lambda i: (0, i))))
        return specs

    outs = pl.pallas_call(
        functools.partial(_inproj_kernel, tn=tn),
        grid=(rows // tm,),
        in_specs=[
            pl.BlockSpec((tm, d), lambda i: (i, 0)),
            pl.BlockSpec((1, d), lambda i: (0, 0)),
            pl.BlockSpec(w_in.shape, lambda i: (0, 0), pipeline_mode=pl.Buffered(1)),
            pl.BlockSpec((tm, HEAD_W), lambda i: (i % nb, 0)),
            pl.BlockSpec((tm, HEAD_W), lambda i: (i % nb, 0)),
            pl.BlockSpec((1, HEAD_W), lambda i: (0, 0)),
            pl.BlockSpec((1, HEAD_W), lambda i: (0, 0)),
        ] + cast_specs(),
        out_specs=[pl.BlockSpec((1, N_GROUPS, tm, HEAD_W), lambda i: (i // nb, 0, i % nb, 0))] + cast_specs(),
        out_shape=[jax.ShapeDtypeStruct((batch, N_GROUPS, seq, HEAD_W), bf16)]
        + [jax.ShapeDtypeStruct(arr.shape, bf16) for arr, _ in cast],
        compiler_params=pltpu.CompilerParams(dimension_semantics=("parallel",), vmem_limit_bytes=VMEM_LIMIT),
        name="inproj",
    )(x2d, g_mix, w_in, cos_t, sin_t, qg, kg, *[arr for arr, _ in cast])
    return outs if cast else outs[0]


def _meta_inproj_kernel(x_ref, g_ref, w_ref, cos_ref, sin_ref, qg_ref, kg_ref, o_ref, wb_ref, u_ref, *, tn):
    j = pl.program_id(0)

    @pl.when(j == 0)
    def _():
        xf = x_ref[...]
        ms = jnp.mean(xf * xf, axis=-1, keepdims=True)
        u_ref[...] = (xf * lax.rsqrt(ms + EPS) * g_ref[...]).astype(bf16)

    wb = w_ref[...].astype(bf16)
    wb_ref[...] = wb
    acc = jnp.dot(u_ref[...], wb, preferred_element_type=f32)
    ngrp = tn // HEAD_W
    grp0 = j * ngrp
    is_dq = grp0 < N_HEADS
    is_dqk = grp0 < 2 * N_HEADS
    is_sq = jnp.logical_and(grp0 >= 3 * N_HEADS, grp0 < 4 * N_HEADS)

    @pl.when(is_dqk)
    def _():
        gain = jnp.where(is_dq, qg_ref[...], kg_ref[...])
        scale = jnp.where(is_dq, QK_DIM ** -0.5 * LOG2E, 1.0).astype(f32)
        lane = _lane_iota((1, HEAD_W))
        first = lane < QK_DIM
        low_half = (lane & (QK_DIM // 2)) == 0
        for g in range(ngrp):
            t = acc[:, g * HEAD_W:(g + 1) * HEAD_W]
            ss = t * t
            tot = jnp.sum(ss, axis=-1, keepdims=True)
            lo = jnp.sum(jnp.where(first, ss, 0.0), axis=-1, keepdims=True)
            msq = jnp.where(first, lo, tot - lo) * (1.0 / QK_DIM)
            y = t * lax.rsqrt(msq + EPS) * gain
            rot = jnp.where(low_half, pltpu.roll(y, HEAD_W - QK_DIM // 2, 1), pltpu.roll(y, QK_DIM // 2, 1))
            o_ref[0, g] = ((y * cos_ref[...] + rot * sin_ref[...]) * scale).astype(bf16)

    @pl.when(is_sq)
    def _():
        for g in range(ngrp):
            o_ref[0, g] = (acc[:, g * HEAD_W:(g + 1) * HEAD_W] * (HEAD_W ** -0.5 * LOG2E)).astype(bf16)

    @pl.when(jnp.logical_not(jnp.logical_or(is_dqk, is_sq)))
    def _():
        for g in range(ngrp):
            o_ref[0, g] = acc[:, g * HEAD_W:(g + 1) * HEAD_W].astype(bf16)


def _meta_inproj(meta, g_mix, w_in_f32, cos_t, sin_t, qg, kg, *, tn):
    rows, d = meta.shape
    ngrp = tn // HEAD_W
    return pl.pallas_call(
        functools.partial(_meta_inproj_kernel, tn=tn),
        grid=(w_in_f32.shape[1] // tn,),
        in_specs=[
            pl.BlockSpec((rows, d), lambda j: (0, 0)),
            pl.BlockSpec((1, d), lambda j: (0, 0)),
            pl.BlockSpec((d, tn), lambda j: (0, j)),
            pl.BlockSpec((rows, HEAD_W), lambda j: (0, 0)),
            pl.BlockSpec((rows, HEAD_W), lambda j: (0, 0)),
            pl.BlockSpec((1, HEAD_W), lambda j: (0, 0)),
            pl.BlockSpec((1, HEAD_W), lambda j: (0, 0)),
        ],
        out_specs=[pl.BlockSpec((1, ngrp, rows, HEAD_W), lambda j: (0, j, 0, 0)),
                   pl.BlockSpec((d, tn), lambda j: (0, j))],
        out_shape=[jax.ShapeDtypeStruct((1, N_GROUPS, rows, HEAD_W), bf16),
                   jax.ShapeDtypeStruct(w_in_f32.shape, bf16)],
        scratch_shapes=[pltpu.VMEM((rows, d), bf16)],
        compiler_params=pltpu.CompilerParams(dimension_semantics=("arbitrary",), vmem_limit_bytes=VMEM_LIMIT),
        name="meta_inproj",
    )(meta, g_mix, w_in_f32, cos_t, sin_t, qg, kg)


def _attn_specs(seq, t, hps, q_grp, k_grp, v_grp):
    return [
        pl.BlockSpec((1, hps, t, HEAD_W), lambda b, h, i: (b, q_grp // hps + h, i, 0)),
        pl.BlockSpec((1, hps, seq, HEAD_W), lambda b, h, i: (b, k_grp // hps + h, 0, 0)),
        pl.BlockSpec((1, hps, seq, HEAD_W), lambda b, h, i: (b, v_grp // hps + h, 0, 0)),
        pl.BlockSpec((1, hps, META_PAD, HEAD_W), lambda b, h, i: (0, k_grp // hps + h, 0, 0)),
        pl.BlockSpec((1, hps, META_PAD, HEAD_W), lambda b, h, i: (0, v_grp // hps + h, 0, 0)),
    ]


def _attn_call(kernel_fn, name, small_specs, small_args, proj, proj_meta, grp_specs, scratch, *, t, hps):
    batch, _, seq, _ = proj.shape
    nq = seq // t
    return pl.pallas_call(
        functools.partial(kernel_fn, t=t, hps=hps),
        grid=(batch, N_HEADS // hps, nq),
        in_specs=small_specs + grp_specs,
        out_specs=pl.BlockSpec((t, hps * HEAD_W), lambda b, h, i: (b * nq + i, h)),
        out_shape=jax.ShapeDtypeStruct((batch * seq, N_HEADS * HEAD_W), bf16),
        scratch_shapes=scratch,
        compiler_params=pltpu.CompilerParams(
            dimension_semantics=("parallel", "parallel", "arbitrary"), vmem_limit_bytes=VMEM_LIMIT),
        name=name,
    )(*small_args, proj, proj, proj, proj_meta, proj_meta)


def _diff_kernel(lq1_ref, lk1_ref, lq2_ref, lk2_ref, og_ref, q_ref, k_ref, v_ref, km_ref, vm_ref, o_ref,
                 s_meta, s_a, s_b, *, t, hps):
    qi = pl.program_id(2)
    heads = range(hps)
    lane_q = _lane_iota((t, HEAD_W))
    nt_dims = (((1,), (1,)), ((), ()))
    tn_dims = (((0,), (0,)), ((), ()))

    def stacked_q(h):
        q = q_ref[0, h]
        zero = jnp.zeros_like(q)
        return jnp.concatenate([jnp.where(lane_q < QK_DIM, q, zero), jnp.where(lane_q >= QK_DIM, q, zero)], axis=0)

    qs = [stacked_q(h) for h in heads]

    lam = (jnp.exp(jnp.sum(lq1_ref[...] * lk1_ref[...], axis=-1, keepdims=True))
           - jnp.exp(jnp.sum(lq2_ref[...] * lk2_ref[...], axis=-1, keepdims=True))
           + LAMBDA_INIT)

    key = lax.broadcasted_iota(jnp.int32, (t, 2 * t), 0)
    qry = lax.broadcasted_iota(jnp.int32, (t, 2 * t), 1)
    causal = key <= jnp.where(qry >= t, qry - t, qry)

    def produce(buf, kblks, mask, m_in):
        m_out = []
        for h in heads:
            s = lax.dot_general(kblks[h], qs[h], nt_dims, preferred_element_type=f32)
            if mask is not None:
                s = jnp.where(mask, s, NEG_INF)
            buf[h] = s
            m_out.append(jnp.maximum(m_in[h], jnp.max(s, axis=0, keepdims=True)))
        return m_out

    def k_block(j):
        start = pl.multiple_of(j * t, t)
        return [k_ref[0, h, pl.ds(start, t), :] for h in heads]

    def consume(buf, vblks, state):
        m_prev, m_cur, l, acc = state
        ls, accs = [], []
        for h in heads:
            alpha = jnp.exp2(m_prev[h] - m_cur[h])
            p = jnp.exp2(buf[h] - m_cur[h])
            ls.append(alpha * l[h] + jnp.sum(p, axis=0, keepdims=True))
            accs.append(alpha * acc[h]
                        + lax.dot_general(vblks[h], p.astype(bf16), tn_dims, preferred_element_type=f32))
        return ls, accs

    def v_block(j):
        start = pl.multiple_of(j * t, t)
        return [v_ref[0, h, pl.ds(start, t), :] for h in heads]

    def step(cur, nxt, j, state):
        m_next = produce(nxt, k_block(j - 1), None, state[1])
        l, acc = consume(cur, v_block(j), state)
        return (state[1], m_next, l, acc)

    def last_steps(cur, state):
        meta_mask = lax.broadcasted_iota(jnp.int32, (META_PAD, 2 * t), 0) < N_META
        m_meta = produce(s_meta, [km_ref[0, h] for h in heads], meta_mask, state[1])
        l, acc = consume(cur, v_block(0), state)
        return consume(s_meta, [vm_ref[0, h] for h in heads], (state[1], m_meta, l, acc))

    neg = [jnp.full((1, 2 * t), NEG_INF, f32) for _ in heads]
    m_diag = produce(s_a, k_block(qi), causal, neg)
    state = (neg, m_diag, [jnp.zeros((1, 2 * t), f32) for _ in heads],
             [jnp.zeros((HEAD_W, 2 * t), f32) for _ in heads])

    def pair(n, st):
        st = step(s_a, s_b, qi - 2 * n, st)
        return step(s_b, s_a, qi - 2 * n - 1, st)

    state = lax.fori_loop(0, qi // 2, pair, state)

    def odd_tail(st):
        return last_steps(s_b, step(s_a, s_b, 1, st))

    def even_tail(st):
        return last_steps(s_a, st)

    l_fin, acc_fin = lax.cond(qi % 2 == 1, odd_tail, even_tail, state)

    for h in heads:
        o = acc_fin[h] / l_fin[h]
        o = o[:, :t] - lam * o[:, t:]
        ms = jnp.mean(o * o, axis=0, keepdims=True)
        o = (o * lax.rsqrt(ms + EPS)).T
        o_ref[:, h * HEAD_W:(h + 1) * HEAD_W] = (o * og_ref[...] * (1.0 - LAMBDA_INIT)).astype(bf16)


def _diff_attention(proj, proj_meta, lq1, lk1, lq2, lk2, og, *, t):
    vec = lambda n: pl.BlockSpec((1, n), lambda b, h, i: (0, 0))
    return _attn_call(
        _diff_kernel, "diff_attn",
        [vec(QK_DIM), vec(QK_DIM), vec(QK_DIM), vec(QK_DIM), vec(HEAD_W)], (lq1, lk1, lq2, lk2, og),
        proj, proj_meta, _attn_specs(proj.shape[2], t, DIFF_HEADS_PER_STEP, 0, N_HEADS, 2 * N_HEADS),
        [pltpu.VMEM((DIFF_HEADS_PER_STEP, META_PAD, 2 * t), f32),
         pltpu.VMEM((DIFF_HEADS_PER_STEP, t, 2 * t), f32),
         pltpu.VMEM((DIFF_HEADS_PER_STEP, t, 2 * t), f32)],
        t=t, hps=DIFF_HEADS_PER_STEP)


def _sb_kernel(og_ref, q_ref, k_ref, v_ref, km_ref, vm_ref, o_ref, d_meta, d_ab, *, t, hps):
    qi = pl.program_id(2)
    heads = range(hps)
    nt_dims = (((1,), (1,)), ((), ()))
    tn_dims = (((0,), (0,)), ((), ()))
    qh = [q_ref[0, h] for h in heads]

    def later_key_matrix(n):
        r = lax.broadcasted_iota(jnp.int32, (n, n), 0)
        c = lax.broadcasted_iota(jnp.int32, (n, n), 1)
        return (c > r).astype(bf16)

    def scores(kblks, valid):
        zs = [lax.dot_general(kblks[h], qh[h], nt_dims, preferred_element_type=f32) for h in heads]
        if valid is not None:
            zs = [jnp.where(valid, z, NEG_INF) for z in zs]
        return zs

    def produce(buf, zs, lmat):
        nsub = lmat.shape[0]
        subs = range(zs[0].shape[0] // nsub)
        sps, laters = [], []
        for h in heads:
            sp = jnp.maximum(zs[h], 0.0) + jnp.log(1.0 + jnp.exp2(-jnp.abs(zs[h]))) * LOG2E
            buf[h] = zs[h] - sp
            sps.append([sp[i * nsub:i * nsub + 1, :] for i in subs])
            laters.append([jnp.dot(lmat, sp[i * nsub:(i + 1) * nsub].astype(bf16), preferred_element_type=f32)
                           for i in subs])
        totals = []
        for h in heads:
            after = None
            for i in reversed(subs):
                rows = slice(i * nsub, (i + 1) * nsub)
                later = laters[h][i] if after is None else laters[h][i] + after
                buf[h, rows] = buf[h, rows] - later
                sub_total = laters[h][i][0:1, :] + sps[h][i]
                after = sub_total if after is None else after + sub_total
            totals.append(after)
        return totals

    def consume(buf, vblks, run, acc):
        a = [jnp.exp2(buf[h] - run[h]).astype(bf16) for h in heads]
        return [acc[h] + lax.dot_general(vblks[h], a[h], tn_dims, preferred_element_type=f32) for h in heads]

    def k_block(j):
        start = pl.multiple_of(j * t, t)
        return [k_ref[0, h, pl.ds(start, t), :] for h in heads]

    def v_block(j):
        start = pl.multiple_of(j * t, t)
        return [v_ref[0, h, pl.ds(start, t), :] for h in heads]

    lmat_t = later_key_matrix(min(t, SB_CUMSUM_BLOCK))

    def step(cur, nxt, j, state):
        run_c, run_n, acc = state
        zs = scores(k_block(j - 1), None)
        acc = consume(cur, v_block(j), run_c, acc)
        totals = produce(nxt, zs, lmat_t)
        return (run_n, [run_n[h] + totals[h] for h in heads], acc)

    def meta_block(run, acc):
        zs = scores([km_ref[0, h] for h in heads], lax.broadcasted_iota(jnp.int32, (META_PAD, t), 0) < N_META)
        produce(d_meta, zs, later_key_matrix(META_PAD))
        return consume(d_meta, [vm_ref[0, h] for h in heads], run, acc)

    def alive(run):
        lowest = run[0]
        for h in heads[1:]:
            lowest = jnp.minimum(lowest, run[h])
        return jnp.min(lowest) < SB_DEAD_LOG2

    key = lax.broadcasted_iota(jnp.int32, (t, t), 0)
    qry = lax.broadcasted_iota(jnp.int32, (t, t), 1)
    totals = produce(d_ab.at[0], scores(k_block(qi), key < qry), lmat_t)
    state = ([jnp.zeros((1, t), f32) for _ in heads], totals, [jnp.zeros((HEAD_W, t), f32) for _ in heads])

    def one_step(carry):
        p, _, st = carry
        st = step(d_ab.at[p % 2], d_ab.at[(p + 1) % 2], qi - p, st)
        return p + 1, alive(st[1]), st

    p, _, (run_c, run_n, acc) = lax.while_loop(
        lambda carry: jnp.logical_and(carry[0] < qi, carry[1]), one_step, (jnp.int32(0), alive(totals), state))

    acc = lax.cond(alive(run_c), lambda a: consume(d_ab.at[p % 2], v_block(qi - p), run_c, a), lambda a: a, acc)
    acc_fin = lax.cond(jnp.logical_and(p == qi, alive(run_n)), lambda a: meta_block(run_n, a), lambda a: a, acc)
    for h in heads:
        acc = acc_fin[h]
        ms = jnp.mean(acc * acc, axis=0, keepdims=True)
        o = (acc * lax.rsqrt(ms + EPS)).T
        o_ref[:, h * HEAD_W:(h + 1) * HEAD_W] = (o * og_ref[...]).astype(bf16)


def _sb_attention(proj, proj_meta, og, *, t):
    return _attn_call(
        _sb_kernel, "sb_attn", [pl.BlockSpec((1, HEAD_W), lambda b, h, i: (0, 0))], (og,),
        proj, proj_meta, _attn_specs(proj.shape[2], t, SB_HEADS_PER_STEP, 3 * N_HEADS, 4 * N_HEADS, 5 * N_HEADS),
        [pltpu.VMEM((SB_HEADS_PER_STEP, META_PAD, t), f32),
         pltpu.VMEM((2, SB_HEADS_PER_STEP, t, t), f32)],
        t=t, hps=SB_HEADS_PER_STEP)


def _outproj_kernel(x_ref, md_ref, ms_ref, wd_ref, ws_ref, o_ref):
    o_ref[...] = (x_ref[...]
                  + jnp.dot(md_ref[...], wd_ref[...], preferred_element_type=f32)
                  + jnp.dot(ms_ref[...], ws_ref[...], preferred_element_type=f32))


def _outproj(x2d, mixed_diff, mixed_sb, w_out, *, tm):
    rows, d = x2d.shape
    half = mixed_diff.shape[1]
    return pl.pallas_call(
        _outproj_kernel,
        grid=(rows // tm,),
        in_specs=[
            pl.BlockSpec((tm, d), lambda i: (i, 0)),
            pl.BlockSpec((tm, half), lambda i: (i, 0)),
            pl.BlockSpec((tm, half), lambda i: (i, 0)),
            pl.BlockSpec((half, d), lambda i: (0, 0)),
            pl.BlockSpec((half, d), lambda i: (1, 0)),
        ],
        out_specs=pl.BlockSpec((tm, d), lambda i: (i, 0)),
        out_shape=jax.ShapeDtypeStruct((rows, d), f32),
        compiler_params=pltpu.CompilerParams(
            dimension_semantics=("parallel",), vmem_limit_bytes=VMEM_LIMIT),
        name="outproj",
    )(x2d, mixed_diff, mixed_sb, w_out, w_out)


def _mlp_kernel(h_ref, g_ref, wu_ref, wd_ref, o_ref, m_ref):
    @pl.when(pl.program_id(1) == 0)
    def _():
        h = h_ref[...]
        ms = jnp.mean(h * h, axis=-1, keepdims=True)
        m_ref[...] = (h * lax.rsqrt(ms + EPS) * g_ref[...]).astype(bf16)
        o_ref[...] = h

    hid = jnp.dot(m_ref[...], wu_ref[...], preferred_element_type=f32)
    hid = jnp.square(jnp.maximum(hid, 0.0))
    o_ref[...] += jnp.dot(hid.astype(bf16), wd_ref[...], preferred_element_type=f32)


def _mlp(h1, g_mlp, w_up, w_down, *, tm, tf):
    rows, d = h1.shape
    return pl.pallas_call(
        _mlp_kernel,
        grid=(rows // tm, w_up.shape[1] // tf),
        in_specs=[
            pl.BlockSpec((tm, d), lambda i, f: (i, 0)),
            pl.BlockSpec((1, d), lambda i, f: (0, 0)),
            pl.BlockSpec((d, tf), lambda i, f: (0, f)),
            pl.BlockSpec((tf, d), lambda i, f: (f, 0)),
        ],
        out_specs=pl.BlockSpec((tm, d), lambda i, f: (i, 0)),
        out_shape=jax.ShapeDtypeStruct((rows, d), f32),
        scratch_shapes=[pltpu.VMEM((tm, d), bf16)],
        compiler_params=pltpu.CompilerParams(
            dimension_semantics=("parallel", "arbitrary"), vmem_limit_bytes=VMEM_LIMIT),
        name="mlp",
    )(h1, g_mlp, w_up, w_down)


def _rope_tables(n_pos):
    pos = np.arange(n_pos, dtype=np.float64)
    inv = ROPE_THETA ** (-np.arange(0, QK_DIM, 2, dtype=np.float64) / QK_DIM)
    ang = pos[:, None] * inv[None, :]
    cos, sin = np.cos(ang).astype(np.float32), np.sin(ang).astype(np.float32)
    return (jnp.asarray(np.concatenate([cos, cos, cos, cos], axis=-1)),
            jnp.asarray(np.concatenate([-sin, sin, -sin, sin], axis=-1)))


def kernel(x, meta_tokens, g_mix, w_in, q_norm_g, k_norm_g, lambda_q1, lambda_k1, lambda_q2, lambda_k2,
           diff_out_g, sb_out_g, w_out, g_mlp, w_up, w_down):
    batch, seq, d = x.shape
    assert g_mix.shape[0] == 1, "single-layer kernel"
    assert meta_tokens.shape[0] == N_META and seq % DIFF_BLOCK == 0 and seq % SB_BLOCK == 0

    x2d = x.reshape(batch * seq, d)
    qg = jnp.tile(q_norm_g[0], 2)[None, :]
    kg = jnp.tile(k_norm_g[0], 2)[None, :]
    cos_t, sin_t = _rope_tables(N_META + seq)

    proj_meta, w_in_b = _meta_inproj(meta_tokens.astype(f32), g_mix, w_in[0], cos_t[:N_META], sin_t[:N_META], qg, kg,
                                     tn=INPROJ_COLS)
    proj_meta = jnp.pad(proj_meta, ((0, 0), (0, 0), (0, META_PAD - N_META), (0, 0)))
    proj, w_up_b, w_down_b, w_out_b = _inproj(x2d, g_mix, w_in_b, cos_t[N_META:], sin_t[N_META:], qg, kg,
                                              batch=batch, seq=seq, tm=INPROJ_ROWS, tn=INPROJ_COLS,
                                              cast=((w_up[0], 1), (w_down[0], 0), (w_out[0], 0)))

    mixed_diff = _diff_attention(proj, proj_meta, lambda_q1, lambda_k1, lambda_q2, lambda_k2, diff_out_g,
                                 t=DIFF_BLOCK)
    mixed_sb = _sb_attention(proj, proj_meta, sb_out_g, t=SB_BLOCK)

    h1 = _outproj(x2d, mixed_diff, mixed_sb, w_out_b, tm=OUTPROJ_ROWS)
    out = _mlp(h1, g_mlp, w_up_b, w_down_b, tm=MLP_ROWS, tf=MLP_HIDDEN_COLS)
    return out.reshape(batch, seq, d)
```

```python
import functools
import math

import jax
import jax.numpy as jnp
import numpy as np
from jax import lax
from jax.experimental import pallas as pl
from jax.experimental.pallas import tpu as pltpu

N_META = 16
N_HEADS = 8
HEAD_W = 128
QK_DIM = 64
N_GROUPS = 6 * N_HEADS
ROPE_THETA = 10000.0
EPS = 1e-6
NEG_INF = -1e30
LAMBDA_INIT = 0.8 - 0.6 * math.exp(0.0)
LOG2E = math.log2(math.e)

INPROJ_ROWS, INPROJ_COLS = 256, 512
MLP_ROWS, MLP_HIDDEN_COLS = 512, 1024
META_PAD = 128
DIFF_BLOCK = 512
SB_BLOCK = 256
SB_CUMSUM_BLOCK = 256
SB_DEAD_LOG2 = 160.0
DIFF_HEADS_PER_STEP = 4
SB_HEADS_PER_STEP = 8
VMEM_LIMIT = 56 * 1024 * 1024

f32 = jnp.float32
bf16 = jnp.bfloat16


def _lane_iota(shape):
    return lax.broadcasted_iota(jnp.int32, shape, len(shape) - 1)


def _inproj_kernel(x_ref, g_ref, w_ref, cos_ref, sin_ref, qg_ref, kg_ref, *refs, tn):
    n_cast = (len(refs) - 1) // 2
    o_ref = refs[n_cast]
    for src_ref, dst_ref in zip(refs[:n_cast], refs[n_cast + 1:]):
        dst_ref[...] = src_ref[...].astype(bf16)

    xf = x_ref[...]
    ms = jnp.mean(xf * xf, axis=-1, keepdims=True)
    u = (xf * lax.rsqrt(ms + EPS) * g_ref[...]).astype(bf16)

    ngrp = tn // HEAD_W
    lane = _lane_iota((1, HEAD_W))
    first = lane < QK_DIM
    low_half = (lane & (QK_DIM // 2)) == 0
    cos = cos_ref[...]
    sin = sin_ref[...]

    for j in range(w_ref.shape[1] // tn):
        acc = jnp.dot(u, w_ref[:, j * tn:(j + 1) * tn], preferred_element_type=f32)
        for g in range(ngrp):
            grp = j * ngrp + g
            t = acc[:, g * HEAD_W:(g + 1) * HEAD_W]
            if grp < 2 * N_HEADS:
                is_q = grp < N_HEADS
                ss = t * t
                tot = jnp.sum(ss, axis=-1, keepdims=True)
                lo = jnp.sum(jnp.where(first, ss, 0.0), axis=-1, keepdims=True)
                msq = jnp.where(first, lo, tot - lo) * (1.0 / QK_DIM)
                y = t * lax.rsqrt(msq + EPS) * (qg_ref[...] if is_q else kg_ref[...])
                rot = jnp.where(low_half, pltpu.roll(y, HEAD_W - QK_DIM // 2, 1), pltpu.roll(y, QK_DIM // 2, 1))
                t = y * cos + rot * sin
                if is_q:
                    t = t * (QK_DIM ** -0.5 * LOG2E)
            elif 3 * N_HEADS <= grp < 4 * N_HEADS:
                t = t * (HEAD_W ** -0.5 * LOG2E)
            o_ref[0, grp] = t.astype(bf16)


def _inproj(x2d, g_mix, w_in, cos_t, sin_t, qg, kg, *, batch, seq, tm, tn, cast=()):
    rows, d = x2d.shape
    nb = seq // tm
    steps = rows // tm

    def cast_specs():
        specs = []
        for arr, axis in cast:
            blk = tuple(n // steps if a == axis else n for a, n in enumerate(arr.shape))
            specs.append(pl.BlockSpec(blk, (lambda i: (i, 0)) if axis == 0 else (lambda i: (0, i))))
        return specs

    outs = pl.pallas_call(
        functools.partial(_inproj_kernel, tn=tn),
        grid=(rows // tm,),
        in_specs=[
            pl.BlockSpec((tm, d), lambda i: (i, 0)),
            pl.BlockSpec((1, d), lambda i: (0, 0)),
            pl.BlockSpec(w_in.shape, lambda i: (0, 0), pipeline_mode=pl.Buffered(1)),
            pl.BlockSpec((tm, HEAD_W), lambda i: (i % nb, 0)),
            pl.BlockSpec((tm, HEAD_W), lambda i: (i % nb, 0)),
            pl.BlockSpec((1, HEAD_W), lambda i: (0, 0)),
            pl.BlockSpec((1, HEAD_W), lambda i: (0, 0)),
        ] + cast_specs(),
        out_specs=[pl.BlockSpec((1, N_GROUPS, tm, HEAD_W), lambda i: (i // nb, 0, i % nb, 0))] + cast_specs(),
        out_shape=[jax.ShapeDtypeStruct((batch, N_GROUPS, seq, HEAD_W), bf16)]
        + [jax.ShapeDtypeStruct(arr.shape, bf16) for arr, _ in cast],
        compiler_params=pltpu.CompilerParams(dimension_semantics=("parallel",), vmem_limit_bytes=VMEM_LIMIT),
        name="inproj",
    )(x2d, g_mix, w_in, cos_t, sin_t, qg, kg, *[arr for arr, _ in cast])
    return outs if cast else outs[0]


def _meta_inproj_kernel(x_ref, g_ref, w_ref, cos_ref, sin_ref, qg_ref, kg_ref, o_ref, wb_ref, u_ref, *, tn):
    j = pl.program_id(0)

    @pl.when(j == 0)
    def _():
        xf = x_ref[...]
        ms = jnp.mean(xf * xf, axis=-1, keepdims=True)
        u_ref[...] = (xf * lax.rsqrt(ms + EPS) * g_ref[...]).astype(bf16)

    wb = w_ref[...].astype(bf16)
    wb_ref[...] = wb
    acc = jnp.dot(u_ref[...], wb, preferred_element_type=f32)
    ngrp = tn // HEAD_W
    grp0 = j * ngrp
    is_dq = grp0 < N_HEADS
    is_dqk = grp0 < 2 * N_HEADS
    is_sq = jnp.logical_and(grp0 >= 3 * N_HEADS, grp0 < 4 * N_HEADS)

    @pl.when(is_dqk)
    def _():
        gain = jnp.where(is_dq, qg_ref[...], kg_ref[...])
        scale = jnp.where(is_dq, QK_DIM ** -0.5 * LOG2E, 1.0).astype(f32)
        lane = _lane_iota((1, HEAD_W))
        first = lane < QK_DIM
        low_half = (lane & (QK_DIM // 2)) == 0
        for g in range(ngrp):
            t = acc[:, g * HEAD_W:(g + 1) * HEAD_W]
            ss = t * t
            tot = jnp.sum(ss, axis=-1, keepdims=True)
            lo = jnp.sum(jnp.where(first, ss, 0.0), axis=-1, keepdims=True)
            msq = jnp.where(first, lo, tot - lo) * (1.0 / QK_DIM)
            y = t * lax.rsqrt(msq + EPS) * gain
            rot = jnp.where(low_half, pltpu.roll(y, HEAD_W - QK_DIM // 2, 1), pltpu.roll(y, QK_DIM // 2, 1))
            o_ref[0, g] = ((y * cos_ref[...] + rot * sin_ref[...]) * scale).astype(bf16)

    @pl.when(is_sq)
    def _():
        for g in range(ngrp):
            o_ref[0, g] = (acc[:, g * HEAD_W:(g + 1) * HEAD_W] * (HEAD_W ** -0.5 * LOG2E)).astype(bf16)

    @pl.when(jnp.logical_not(jnp.logical_or(is_dqk, is_sq)))
    def _():
        for g in range(ngrp):
            o_ref[0, g] = acc[:, g * HEAD_W:(g + 1) * HEAD_W].astype(bf16)


def _meta_inproj(meta, g_mix, w_in_f32, cos_t, sin_t, qg, kg, *, tn):
    rows, d = meta.shape
    ngrp = tn // HEAD_W
    return pl.pallas_call(
        functools.partial(_meta_inproj_kernel, tn=tn),
        grid=(w_in_f32.shape[1] // tn,),
        in_specs=[
            pl.BlockSpec((rows, d), lambda j: (0, 0)),
            pl.BlockSpec((1, d), lambda j: (0, 0)),
            pl.BlockSpec((d, tn), lambda j: (0, j)),
            pl.BlockSpec((rows, HEAD_W), lambda j: (0, 0)),
            pl.BlockSpec((rows, HEAD_W), lambda j: (0, 0)),
            pl.BlockSpec((1, HEAD_W), lambda j: (0, 0)),
            pl.BlockSpec((1, HEAD_W), lambda j: (0, 0)),
        ],
        out_specs=[pl.BlockSpec((1, ngrp, rows, HEAD_W), lambda j: (0, j, 0, 0)),
                   pl.BlockSpec((d, tn), lambda j: (0, j))],
        out_shape=[jax.ShapeDtypeStruct((1, N_GROUPS, rows, HEAD_W), bf16),
                   jax.ShapeDtypeStruct(w_in_f32.shape, bf16)],
        scratch_shapes=[pltpu.VMEM((rows, d), bf16)],
        compiler_params=pltpu.CompilerParams(dimension_semantics=("arbitrary",), vmem_limit_bytes=VMEM_LIMIT),
        name="meta_inproj",
    )(meta, g_mix, w_in_f32, cos_t, sin_t, qg, kg)


def _attn_specs(seq, t, hps, q_grp, k_grp, v_grp):
    return [
        pl.BlockSpec((1, hps, t, HEAD_W), lambda b, h, i: (b, q_grp // hps + h, i, 0)),
        pl.BlockSpec((1, hps, seq, HEAD_W), lambda b, h, i: (b, k_grp // hps + h, 0, 0)),
        pl.BlockSpec((1, hps, seq, HEAD_W), lambda b, h, i: (b, v_grp // hps + h, 0, 0)),
        pl.BlockSpec((1, hps, META_PAD, HEAD_W), lambda b, h, i: (0, k_grp // hps + h, 0, 0)),
        pl.BlockSpec((1, hps, META_PAD, HEAD_W), lambda b, h, i: (0, v_grp // hps + h, 0, 0)),
    ]


def _attn_call(kernel_fn, name, small_specs, small_args, proj, proj_meta, grp_specs, scratch, *, t, hps):
    batch, _, seq, _ = proj.shape
    nq = seq // t
    return pl.pallas_call(
        functools.partial(kernel_fn, t=t, hps=hps),
        grid=(batch, N_HEADS // hps, nq),
        in_specs=small_specs + grp_specs,
        out_specs=pl.BlockSpec((t, hps * HEAD_W), lambda b, h, i: (b * nq + i, h)),
        out_shape=jax.ShapeDtypeStruct((batch * seq, N_HEADS * HEAD_W), bf16),
        scratch_shapes=scratch,
        compiler_params=pltpu.CompilerParams(
            dimension_semantics=("parallel", "parallel", "arbitrary"), vmem_limit_bytes=VMEM_LIMIT),
        name=name,
    )(*small_args, proj, proj, proj, proj_meta, proj_meta)


def _diff_kernel(lq1_ref, lk1_ref, lq2_ref, lk2_ref, og_ref, q_ref, k_ref, v_ref, km_ref, vm_ref, o_ref,
                 s_meta, s_a, s_b, *, t, hps):
    qi = pl.program_id(2)
    heads = range(hps)
    lane_q = _lane_iota((t, HEAD_W))
    nt_dims = (((1,), (1,)), ((), ()))
    tn_dims = (((0,), (0,)), ((), ()))

    def stacked_q(h):
        q = q_ref[0, h]
        zero = jnp.zeros_like(q)
        return jnp.concatenate([jnp.where(lane_q < QK_DIM, q, zero), jnp.where(lane_q >= QK_DIM, q, zero)], axis=0)

    qs = [stacked_q(h) for h in heads]

    lam = (jnp.exp(jnp.sum(lq1_ref[...] * lk1_ref[...], axis=-1, keepdims=True))
           - jnp.exp(jnp.sum(lq2_ref[...] * lk2_ref[...], axis=-1, keepdims=True))
           + LAMBDA_INIT)

    key = lax.broadcasted_iota(jnp.int32, (t, 2 * t), 0)
    qry = lax.broadcasted_iota(jnp.int32, (t, 2 * t), 1)
    causal = key <= jnp.where(qry >= t, qry - t, qry)

    def produce(buf, kblks, mask, m_in):
        m_out = []
        for h in heads:
            s = lax.dot_general(kblks[h], qs[h], nt_dims, preferred_element_type=f32)
            if mask is not None:
                s = jnp.where(mask, s, NEG_INF)
            buf[h] = s
            m_out.append(jnp.maximum(m_in[h], jnp.max(s, axis=0, keepdims=True)))
        return m_out

    def k_block(j):
        start = pl.multiple_of(j * t, t)
        return [k_ref[0, h, pl.ds(start, t), :] for h in heads]

    def consume(buf, vblks, state):
        m_prev, m_cur, l, acc = state
        ls, accs = [], []
        for h in heads:
            alpha = jnp.exp2(m_prev[h] - m_cur[h])
            p = jnp.exp2(buf[h] - m_cur[h])
            ls.append(alpha * l[h] + jnp.sum(p, axis=0, keepdims=True))
            accs.append(alpha * acc[h]
                        + lax.dot_general(vblks[h], p.astype(bf16), tn_dims, preferred_element_type=f32))
        return ls, accs

    def v_block(j):
        start = pl.multiple_of(j * t, t)
        return [v_ref[0, h, pl.ds(start, t), :] for h in heads]

    def step(cur, nxt, j, state):
        m_next = produce(nxt, k_block(j - 1), None, state[1])
        l, acc = consume(cur, v_block(j), state)
        return (state[1], m_next, l, acc)

    def last_steps(cur, state):
        meta_mask = lax.broadcasted_iota(jnp.int32, (META_PAD, 2 * t), 0) < N_META
        m_meta = produce(s_meta, [km_ref[0, h] for h in heads], meta_mask, state[1])
        l, acc = consume(cur, v_block(0), state)
        return consume(s_meta, [vm_ref[0, h] for h in heads], (state[1], m_meta, l, acc))

    neg = [jnp.full((1, 2 * t), NEG_INF, f32) for _ in heads]
    m_diag = produce(s_a, k_block(qi), causal, neg)
    state = (neg, m_diag, [jnp.zeros((1, 2 * t), f32) for _ in heads],
             [jnp.zeros((HEAD_W, 2 * t), f32) for _ in heads])

    def pair(n, st):
        st = step(s_a, s_b, qi - 2 * n, st)
        return step(s_b, s_a, qi - 2 * n - 1, st)

    state = lax.fori_loop(0, qi // 2, pair, state)

    def odd_tail(st):
        return last_steps(s_b, step(s_a, s_b, 1, st))

    def even_tail(st):
        return last_steps(s_a, st)

    l_fin, acc_fin = lax.cond(qi % 2 == 1, odd_tail, even_tail, state)

    for h in heads:
        o = acc_fin[h] / l_fin[h]
        o = o[:, :t] - lam * o[:, t:]
        ms = jnp.mean(o * o, axis=0, keepdims=True)
        o = (o * lax.rsqrt(ms + EPS)).T
        o_ref[:, h * HEAD_W:(h + 1) * HEAD_W] = (o * og_ref[...] * (1.0 - LAMBDA_INIT)).astype(bf16)


def _diff_attention(proj, proj_meta, lq1, lk1, lq2, lk2, og, *, t):
    vec = lambda n: pl.BlockSpec((1, n), lambda b, h, i: (0, 0))
    return _attn_call(
        _diff_kernel, "diff_attn",
        [vec(QK_DIM), vec(QK_DIM), vec(QK_DIM), vec(QK_DIM), vec(HEAD_W)], (lq1, lk1, lq2, lk2, og),
        proj, proj_meta, _attn_specs(proj.shape[2], t, DIFF_HEADS_PER_STEP, 0, N_HEADS, 2 * N_HEADS),
        [pltpu.VMEM((DIFF_HEADS_PER_STEP, META_PAD, 2 * t), f32),
         pltpu.VMEM((DIFF_HEADS_PER_STEP, t, 2 * t), f32),
         pltpu.VMEM((DIFF_HEADS_PER_STEP, t, 2 * t), f32)],
        t=t, hps=DIFF_HEADS_PER_STEP)


def _sb_kernel(og_ref, q_ref, k_ref, v_ref, km_ref, vm_ref, o_ref, d_meta, d_ab, *, t, hps):
    qi = pl.program_id(2)
    heads = range(hps)
    nt_dims = (((1,), (1,)), ((), ()))
    tn_dims = (((0,), (0,)), ((), ()))
    qh = [q_ref[0, h] for h in heads]

    def later_key_matrix(n):
        r = lax.broadcasted_iota(jnp.int32, (n, n), 0)
        c = lax.broadcasted_iota(jnp.int32, (n, n), 1)
        return (c > r).astype(bf16)

    def scores(kblks, valid):
        zs = [lax.dot_general(kblks[h], qh[h], nt_dims, preferred_element_type=f32) for h in heads]
        if valid is not None:
            zs = [jnp.where(valid, z, NEG_INF) for z in zs]
        return zs

    def produce(buf, zs, lmat):
        nsub = lmat.shape[0]
        subs = range(zs[0].shape[0] // nsub)
        sps, laters = [], []
        for h in heads:
            sp = jnp.maximum(zs[h], 0.0) + jnp.log(1.0 + jnp.exp2(-jnp.abs(zs[h]))) * LOG2E
            buf[h] = zs[h] - sp
            sps.append([sp[i * nsub:i * nsub + 1, :] for i in subs])
            laters.append([jnp.dot(lmat, sp[i * nsub:(i + 1) * nsub].astype(bf16), preferred_element_type=f32)
                           for i in subs])
        totals = []
        for h in heads:
            after = None
            for i in reversed(subs):
                rows = slice(i * nsub, (i + 1) * nsub)
                later = laters[h][i] if after is None else laters[h][i] + after
                buf[h, rows] = buf[h, rows] - later
                sub_total = laters[h][i][0:1, :] + sps[h][i]
                after = sub_total if after is None else after + sub_total
            totals.append(after)
        return totals

    def consume(buf, vblks, run, acc):
        a = [jnp.exp2(buf[h] - run[h]).astype(bf16) for h in heads]
        return [acc[h] + lax.dot_general(vblks[h], a[h], tn_dims, preferred_element_type=f32) for h in heads]

    def k_block(j):
        start = pl.multiple_of(j * t, t)
        return [k_ref[0, h, pl.ds(start, t), :] for h in heads]

    def v_block(j):
        start = pl.multiple_of(j * t, t)
        return [v_ref[0, h, pl.ds(start, t), :] for h in heads]

    lmat_t = later_key_matrix(min(t, SB_CUMSUM_BLOCK))

    def step(cur, nxt, j, state):
        run_c, run_n, acc = state
        zs = scores(k_block(j - 1), None)
        acc = consume(cur, v_block(j), run_c, acc)
        totals = produce(nxt, zs, lmat_t)
        return (run_n, [run_n[h] + totals[h] for h in heads], acc)

    def meta_block(run, acc):
        zs = scores([km_ref[0, h] for h in heads], lax.broadcasted_iota(jnp.int32, (META_PAD, t), 0) < N_META)
        produce(d_meta, zs, later_key_matrix(META_PAD))
        return consume(d_meta, [vm_ref[0, h] for h in heads], run, acc)

    def alive(run):
        lowest = run[0]
        for h in heads[1:]:
            lowest = jnp.minimum(lowest, run[h])
        return jnp.min(lowest) < SB_DEAD_LOG2

    key = lax.broadcasted_iota(jnp.int32, (t, t), 0)
    qry = lax.broadcasted_iota(jnp.int32, (t, t), 1)
    totals = produce(d_ab.at[0], scores(k_block(qi), key < qry), lmat_t)
    state = ([jnp.zeros((1, t), f32) for _ in heads], totals, [jnp.zeros((HEAD_W, t), f32) for _ in heads])

    def one_step(carry):
        p, _, st = carry
        st = step(d_ab.at[p % 2], d_ab.at[(p + 1) % 2], qi - p, st)
        return p + 1, alive(st[1]), st

    p, _, (run_c, run_n, acc) = lax.while_loop(
        lambda carry: jnp.logical_and(carry[0] < qi, carry[1]), one_step, (jnp.int32(0), alive(totals), state))

    acc = lax.cond(alive(run_c), lambda a: consume(d_ab.at[p % 2], v_block(qi - p), run_c, a), lambda a: a, acc)
    acc_fin = lax.cond(jnp.logical_and(p == qi, alive(run_n)), lambda a: meta_block(run_n, a), lambda a: a, acc)
    for h in heads:
        acc = acc_fin[h]
        ms = jnp.mean(acc * acc, axis=0, keepdims=True)
        o = (acc * lax.rsqrt(ms + EPS)).T
        o_ref[:, h * HEAD_W:(h + 1) * HEAD_W] = (o * og_ref[...]).astype(bf16)


def _sb_attention(proj, proj_meta, og, *, t):
    return _attn_call(
        _sb_kernel, "sb_attn", [pl.BlockSpec((1, HEAD_W), lambda b, h, i: (0, 0))], (og,),
        proj, proj_meta, _attn_specs(proj.shape[2], t, SB_HEADS_PER_STEP, 3 * N_HEADS, 4 * N_HEADS, 5 * N_HEADS),
        [pltpu.VMEM((SB_HEADS_PER_STEP, META_PAD, t), f32),
         pltpu.VMEM((2, SB_HEADS_PER_STEP, t, t), f32)],
        t=t, hps=SB_HEADS_PER_STEP)


def _outproj_mlp_kernel(x_ref, md_ref, ms_ref, wod_ref, wos_ref, g_ref, wu_ref, wd_ref, o_ref, m_ref):
    @pl.when(pl.program_id(1) == 0)
    def _():
        h = (x_ref[...]
             + jnp.dot(md_ref[...], wod_ref[...], preferred_element_type=f32)
             + jnp.dot(ms_ref[...], wos_ref[...], preferred_element_type=f32))
        ms = jnp.mean(h * h, axis=-1, keepdims=True)
        m_ref[...] = (h * lax.rsqrt(ms + EPS) * g_ref[...]).astype(bf16)
        o_ref[...] = h

    hid = jnp.dot(m_ref[...], wu_ref[...], preferred_element_type=f32)
    hid = jnp.square(jnp.maximum(hid, 0.0))
    o_ref[...] += jnp.dot(hid.astype(bf16), wd_ref[...], preferred_element_type=f32)


def _outproj_mlp(x2d, mixed_diff, mixed_sb, w_out, g_mlp, w_up, w_down, *, tm, tf):
    rows, d = x2d.shape
    half = mixed_diff.shape[1]
    return pl.pallas_call(
        _outproj_mlp_kernel,
        grid=(rows // tm, w_up.shape[1] // tf),
        in_specs=[
            pl.BlockSpec((tm, d), lambda i, f: (i, 0)),
            pl.BlockSpec((tm, half), lambda i, f: (i, 0)),
            pl.BlockSpec((tm, half), lambda i, f: (i, 0)),
            pl.BlockSpec((half, d), lambda i, f: (0, 0), pipeline_mode=pl.Buffered(1)),
            pl.BlockSpec((half, d), lambda i, f: (1, 0), pipeline_mode=pl.Buffered(1)),
            pl.BlockSpec((1, d), lambda i, f: (0, 0)),
            pl.BlockSpec((d, tf), lambda i, f: (0, f)),
            pl.BlockSpec((tf, d), lambda i, f: (f, 0)),
        ],
        out_specs=pl.BlockSpec((tm, d), lambda i, f: (i, 0)),
        out_shape=jax.ShapeDtypeStruct((rows, d), f32),
        scratch_shapes=[pltpu.VMEM((tm, d), bf16)],
        compiler_params=pltpu.CompilerParams(
            dimension_semantics=("parallel", "arbitrary"), vmem_limit_bytes=VMEM_LIMIT),
        name="outproj_mlp",
    )(x2d, mixed_diff, mixed_sb, w_out, w_out, g_mlp, w_up, w_down)


def _rope_tables(n_pos):
    pos = np.arange(n_pos, dtype=np.float64)
    inv = ROPE_THETA ** (-np.arange(0, QK_DIM, 2, dtype=np.float64) / QK_DIM)
    ang = pos[:, None] * inv[None, :]
    cos, sin = np.cos(ang).astype(np.float32), np.sin(ang).astype(np.float32)
    return (jnp.asarray(np.concatenate([cos, cos, cos, cos], axis=-1)),
            jnp.asarray(np.concatenate([-sin, sin, -sin, sin], axis=-1)))


def kernel(x, meta_tokens, g_mix, w_in, q_norm_g, k_norm_g, lambda_q1, lambda_k1, lambda_q2, lambda_k2,
           diff_out_g, sb_out_g, w_out, g_mlp, w_up, w_down):
    batch, seq, d = x.shape
    assert g_mix.shape[0] == 1, "single-layer kernel"
    assert meta_tokens.shape[0] == N_META and seq % DIFF_BLOCK == 0 and seq % SB_BLOCK == 0

    x2d = x.reshape(batch * seq, d)
    qg = jnp.tile(q_norm_g[0], 2)[None, :]
    kg = jnp.tile(k_norm_g[0], 2)[None, :]
    cos_t, sin_t = _rope_tables(N_META + seq)

    proj_meta, w_in_b = _meta_inproj(meta_tokens.astype(f32), g_mix, w_in[0], cos_t[:N_META], sin_t[:N_META], qg, kg,
                                     tn=INPROJ_COLS)
    proj_meta = jnp.pad(proj_meta, ((0, 0), (0, 0), (0, META_PAD - N_META), (0, 0)))
    proj, w_up_b, w_down_b, w_out_b = _inproj(x2d, g_mix, w_in_b, cos_t[N_META:], sin_t[N_META:], qg, kg,
                                              batch=batch, seq=seq, tm=INPROJ_ROWS, tn=INPROJ_COLS,
                                              cast=((w_up[0], 1), (w_down[0], 0), (w_out[0], 0)))

    mixed_diff = _diff_attention(proj, proj_meta, lambda_q1, lambda_k1, lambda_q2, lambda_k2, diff_out_g,
                                 t=DIFF_BLOCK)
    mixed_sb = _sb_attention(proj, proj_meta, sb_out_g, t=SB_BLOCK)

    out = _outproj_mlp(x2d, mixed_diff, mixed_sb, w_out_b, g_mlp, w_up_b, w_down_b, tm=MLP_ROWS, tf=MLP_HIDDEN_COLS)
    return out.reshape(batch, seq, d)
```
